```python
import jax
import jax.numpy as jnp
from jax import lax
import numpy as np

D_MODEL = 1024
BATCH = 8
SEQ = 4096
DEPTH = 2
DEC_BATCH = 32
DEC_SEQ = 4
PAST_LEN = 16384
PAGE_SIZE = 128

N_EVEN = (DEPTH + 1) // 2
N_ODD = DEPTH // 2
W_A = D_MODEL // 2
G_A = 8
DG_A = W_A // G_A
CHUNK = 128
W_B = D_MODEL // 2
H_B = 8
DH_B = W_B // H_B
Q_BLOCK = 128
SB_BIAS_INIT = -7.0
H_C = 4
DK_C = D_MODEL // H_C
DV_C = 2 * DK_C
RET_CHUNK = 128
ROPE_BASE = 10000.0
D_FF = ((8 * D_MODEL // 3 + 127) // 128) * 128
CONV_W = 3
ALPHA = (2 * DEPTH) ** 0.25
BETA = (8 * DEPTH) ** -0.25
LN_EPS = 1e-5

kernel_name = 'hybrid_gmlp_stickbreak_retention_step'


def _layer_norm(x, g, b):
    xf = x.astype(jnp.float32)
    mu = jnp.mean(xf, axis=-1, keepdims=True)
    var = jnp.mean(jnp.square(xf - mu), axis=-1, keepdims=True)
    y = (xf - mu) * lax.rsqrt(var + LN_EPS)
    return (y * g.astype(jnp.float32) + b.astype(jnp.float32)).astype(x.dtype)


def _rotary(x, pos):
    half = x.shape[-1] // 2
    inv = ROPE_BASE ** (-jnp.arange(half, dtype=jnp.float32) / half)
    ang = pos.astype(jnp.float32)[:, None] * inv[None, :]
    cos = jnp.cos(ang)[None, :, None, :]
    sin = jnp.sin(ang)[None, :, None, :]
    xf = x.astype(jnp.float32)
    x1, x2 = xf[..., :half], xf[..., half:]
    return jnp.concatenate([x1 * cos - x2 * sin, x1 * sin + x2 * cos], axis=-1).astype(x.dtype)


def _chunk_gmlp(u, v, w_s, b_s):
    B, T, G, dg = v.shape
    L = min(CHUNK, T)
    n = -(-T // L)
    pad = n * L - T
    w = jnp.tril(w_s[:, :L, :L])
    b = b_s[:, :L]
    vp = jnp.pad(v, ((0, 0), (0, pad), (0, 0), (0, 0))).reshape(B, n, L, G, dg)
    mixed = jnp.einsum('gts,bnsgd->bntgd', w, vp) + b.T[None, None, :, :, None]
    mixed = mixed.reshape(B, n * L, G, dg)[:, :T]
    return u * mixed


def _sb_weights(z, causal):
    log_keep = jnp.where(causal, -jax.nn.softplus(z), 0.0)
    shifted = jnp.concatenate([log_keep[..., 1:], jnp.zeros_like(log_keep[..., :1])], axis=-1)
    suffix = lax.cumsum(shifted, axis=3, reverse=True)
    return jnp.where(causal, jnp.exp(jax.nn.log_sigmoid(z) + suffix), 0.0)


def _sb_block(q, k, v, qpos, kpos, bias):
    scale = q.shape[-1] ** -0.5
    z = (jnp.einsum('bqhd,bkhd->bhqk', q.astype(jnp.float32), k.astype(jnp.float32)) * scale
         + bias.astype(jnp.float32)[None, :, None, None])
    a = _sb_weights(z, kpos[None, :] < qpos[:, None])
    return jnp.einsum('bhqk,bkhd->bqhd', a, v.astype(jnp.float32)).astype(q.dtype)


def _sb_self(q, k, v, bias):
    B, T, H, d = q.shape
    kpos = jnp.arange(T)
    if T <= Q_BLOCK:
        return _sb_block(q, k, v, kpos, kpos, bias)
    nb = T // Q_BLOCK
    qb = jnp.moveaxis(q.reshape(B, nb, Q_BLOCK, H, d), 1, 0)
    starts = jnp.arange(nb) * Q_BLOCK

    def one(args):
        qi, s = args
        return _sb_block(qi, k, v, s + jnp.arange(Q_BLOCK), kpos, bias)

    out = lax.map(one, (qb, starts))
    return jnp.moveaxis(out, 0, 1).reshape(B, T, H, d)


def _sb_decode(q, k, v, past_k, past_v, bias):
    P = past_k.shape[1]
    T = q.shape[1]
    scale = q.shape[-1] ** -0.5
    qf = q.astype(jnp.float32)
    z = (jnp.concatenate([
        jnp.einsum('bqhd,bkhd->bhqk', qf, past_k.astype(jnp.float32)),
        jnp.einsum('bqhd,bkhd->bhqk', qf, k.astype(jnp.float32))], axis=-1) * scale
         + bias.astype(jnp.float32)[None, :, None, None])
    qpos = P + jnp.arange(T)
    kpos = jnp.arange(P + T)
    a = _sb_weights(z, kpos[None, :] < qpos[:, None])
    out = (jnp.einsum('bhqk,bkhd->bqhd', a[..., :P], past_v.astype(jnp.float32))
           + jnp.einsum('bhqk,bkhd->bqhd', a[..., P:], v.astype(jnp.float32)))
    return out.astype(q.dtype)


def _ab_mixer(x, w_in, vln_g, vln_b, w_s, b_s, sb_bias, w_out, past_k, past_v):
    B, T, _ = x.shape
    h = x @ w_in
    u_a, v_a, q, k, v = jnp.split(h, [W_A, 2 * W_A, 2 * W_A + W_B, 2 * W_A + 2 * W_B], axis=-1)
    u_a = jax.nn.gelu(u_a).reshape(B, T, G_A, DG_A)
    v_a = _layer_norm(jax.nn.gelu(v_a), vln_g, vln_b).reshape(B, T, G_A, DG_A)
    a_out = _chunk_gmlp(u_a, v_a, w_s, b_s).reshape(B, T, W_A)
    q = q.reshape(B, T, H_B, DH_B)
    k = k.reshape(B, T, H_B, DH_B)
    v = v.reshape(B, T, H_B, DH_B)
    if past_k is None:
        b_out = _sb_self(q, k, v, sb_bias)
    else:
        b_out = _sb_decode(q, k, v, past_k, past_v, sb_bias)
    y = jnp.concatenate([a_out, b_out.reshape(B, T, W_B)], axis=-1) @ w_out
    return y, k, v, v_a


def _retention(q, k, v, s0):
    B, T, H, dk = q.shape
    dv = v.shape[-1]
    L = RET_CHUNK if T % RET_CHUNK == 0 else T
    n = T // L
    log_g = jnp.log1p(-jnp.power(2.0, -5.0 - jnp.arange(H, dtype=jnp.float32)))
    idx = jnp.arange(L, dtype=jnp.float32)
    diff = idx[:, None] - idx[None, :]
    causal = diff >= 0
    dmat = jnp.where(causal[None], jnp.exp(jnp.where(causal, diff, 0.0)[None] * log_g[:, None, None]), 0.0)
    q_dec = jnp.exp((idx[None, :] + 1.0) * log_g[:, None]).T[None, :, :, None]
    k_dec = jnp.exp((L - 1.0 - idx)[None, :] * log_g[:, None]).T[None, :, :, None]
    c_dec = jnp.exp(L * log_g)[None, :, None, None]

    def to_chunks(a):
        return jnp.moveaxis(a.astype(jnp.float32).reshape(B, n, L, H, a.shape[-1]), 1, 0)

    def step(S, inp):
        qc, kc, vc = inp
        sc = jnp.einsum('blhd,bmhd->bhlm', qc, kc) * dmat[None]
        o = jnp.einsum('bhlm,bmhe->blhe', sc, vc) + jnp.einsum('blhd,bhde->blhe', qc, S) * q_dec
        S = S * c_dec + jnp.einsum('bmhd,bmhe->bhde', kc * k_dec, vc)
        return S, o

    S, o = lax.scan(step, s0.astype(jnp.float32), (to_chunks(q), to_chunks(k), to_chunks(v)))
    o = jnp.moveaxis(o, 0, 1).reshape(B, T, H, dv)
    return o.astype(q.dtype), S


def _head_norm(o, g, b):
    B, T, H, dv = o.shape
    of = o.astype(jnp.float32)
    mu = jnp.mean(of, axis=-1, keepdims=True)
    var = jnp.mean(jnp.square(of - mu), axis=-1, keepdims=True)
    y = ((of - mu) * lax.rsqrt(var + LN_EPS)).reshape(B, T, H * dv)
    return (y * g.astype(jnp.float32) + b.astype(jnp.float32)).astype(o.dtype)


def _ret_mixer(x, w_in, gn_g, gn_b, w_out, s0, pos0):
    B, T, _ = x.shape
    h = x @ w_in
    q, k, v, g = jnp.split(h, [H_C * DK_C, 2 * H_C * DK_C, 2 * H_C * DK_C + H_C * DV_C], axis=-1)
    pos = pos0 + jnp.arange(T)
    q = _rotary(q.reshape(B, T, H_C, DK_C), pos)
    k = _rotary(k.reshape(B, T, H_C, DK_C), pos) * (DK_C ** -0.5)
    v = v.reshape(B, T, H_C, DV_C)
    o, S = _retention(q, k, v, s0)
    y = (jax.nn.silu(g) * _head_norm(o, gn_g, gn_b)) @ w_out
    return y, S


def _conv_ffn(x, w_up, conv_w, conv_b, w_down, conv_state):
    T = x.shape[1]
    h = x @ w_up
    g, val = jnp.split(h, 2, axis=-1)
    gp = jnp.concatenate([conv_state.astype(g.dtype), g], axis=1)
    c = conv_b
    for i in range(CONV_W):
        c = c + conv_w[i] * gp[:, i:i + T]
    y = (jax.nn.gelu(c) * val) @ w_down
    return y, gp[:, T:]


def setup_inputs(seed: int = 0) -> dict:
    key = jax.random.key(seed)
    ks = jax.random.split(key, 32)
    n_pages = PAST_LEN // PAGE_SIZE
    n_used = DEC_BATCH * n_pages
    n_phys = n_used + n_used // 4

    def nrm(k, shape, s):
        return jax.random.normal(k, shape, jnp.float32) * s

    d_mix = W_A + W_B
    d_ret_in = 2 * H_C * DK_C + 2 * H_C * DV_C
    page_table = jax.random.permutation(ks[6], n_phys)[:n_used].reshape(DEC_BATCH, n_pages).astype(jnp.int32)
    return {
        'x_prompt': nrm(ks[0], (BATCH, SEQ, D_MODEL), 1.0),
        'x_sample': nrm(ks[1], (DEC_BATCH, DEC_SEQ, D_MODEL), 1.0),
        'cache_sb_k': nrm(ks[2], (N_EVEN, n_phys, PAGE_SIZE, H_B, DH_B), 1.0),
        'cache_sb_v': nrm(ks[3], (N_EVEN, n_phys, PAGE_SIZE, H_B, DH_B), 1.0),
        'state_ret': nrm(ks[4], (N_ODD, DEC_BATCH, H_C, DK_C, DV_C), 0.5),
        'state_ffn_conv': nrm(ks[5], (DEPTH, DEC_BATCH, CONV_W - 1, D_FF), 1.0),
        'page_table': page_table,
        'w_in_ab': nrm(ks[7], (N_EVEN, D_MODEL, 2 * W_A + 3 * W_B), D_MODEL ** -0.5),
        'vln_g': 1.0 + nrm(ks[8], (N_EVEN, W_A), 0.02),
        'vln_b': nrm(ks[9], (N_EVEN, W_A), 0.02),
        'w_s': nrm(ks[10], (N_EVEN, G_A, CHUNK, CHUNK), CHUNK ** -0.5),
        'b_s': 1.0 + nrm(ks[11], (N_EVEN, G_A, CHUNK), 0.02),
        'sb_bias': SB_BIAS_INIT + nrm(ks[25], (N_EVEN, H_B), 0.1),
        'w_out_ab': nrm(ks[12], (N_EVEN, d_mix, D_MODEL), BETA * d_mix ** -0.5),
        'w_in_ret': nrm(ks[13], (N_ODD, D_MODEL, d_ret_in), D_MODEL ** -0.5),
        'gn_g': 1.0 + nrm(ks[14], (N_ODD, H_C * DV_C), 0.02),
        'gn_b': nrm(ks[15], (N_ODD, H_C * DV_C), 0.02),
        'w_out_ret': nrm(ks[16], (N_ODD, H_C * DV_C, D_MODEL), BETA * (H_C * DV_C) ** -0.5),
        'ln1_g': 1.0 + nrm(ks[17], (DEPTH, D_MODEL), 0.02),
        'ln1_b': nrm(ks[18], (DEPTH, D_MODEL), 0.02),
        'ln2_g': 1.0 + nrm(ks[19], (DEPTH, D_MODEL), 0.02),
        'ln2_b': nrm(ks[20], (DEPTH, D_MODEL), 0.02),
        'w_up': nrm(ks[21], (DEPTH, D_MODEL, 2 * D_FF), D_MODEL ** -0.5),
        'conv_w': nrm(ks[22], (DEPTH, CONV_W, D_FF), CONV_W ** -0.5),
        'conv_b': nrm(ks[23], (DEPTH, D_FF), 0.02),
        'w_down': nrm(ks[24], (DEPTH, D_FF, D_MODEL), BETA * D_FF ** -0.5),
    }


def reference(x_prompt, x_sample, cache_sb_k, cache_sb_v, state_ret, state_ffn_conv, page_table,
              w_in_ab, vln_g, vln_b, w_s, b_s, sb_bias, w_out_ab, w_in_ret, gn_g, gn_b, w_out_ret,
              ln1_g, ln1_b, ln2_g, ln2_b, w_up, conv_w, conv_b, w_down):
    dec_b, n_pages = page_table.shape
    past_len = n_pages * PAGE_SIZE
    xp, xs = x_prompt, x_sample
    bp, tp = xp.shape[0], xp.shape[1]
    sb_k_p, sb_v_p, sb_k_s, sb_v_s, chunk_v_s = [], [], [], [], []
    ret_p, ret_s, conv_p, conv_s = [], [], [], []
    for l in range(DEPTH):
        if l % 2 == 0:
            e = l // 2
            past_k = cache_sb_k[e][page_table].reshape(dec_b, past_len, H_B, DH_B)
            past_v = cache_sb_v[e][page_table].reshape(dec_b, past_len, H_B, DH_B)
            yp, kp, vp, _ = _ab_mixer(xp, w_in_ab[e], vln_g[e], vln_b[e], w_s[e], b_s[e], sb_bias[e], w_out_ab[e], None, None)
            ys, kn, vn, va = _ab_mixer(xs, w_in_ab[e], vln_g[e], vln_b[e], w_s[e], b_s[e], sb_bias[e], w_out_ab[e], past_k, past_v)
            sb_k_p.append(kp.reshape(bp, tp // PAGE_SIZE, PAGE_SIZE, H_B, DH_B))
            sb_v_p.append(vp.reshape(bp, tp // PAGE_SIZE, PAGE_SIZE, H_B, DH_B))
            sb_k_s.append(kn)
            sb_v_s.append(vn)
            chunk_v_s.append(va)
        else:
            o = l // 2
            s0 = jnp.zeros((bp, H_C, DK_C, DV_C), jnp.float32)
            yp, sp = _ret_mixer(xp, w_in_ret[o], gn_g[o], gn_b[o], w_out_ret[o], s0, 0)
            ys, ss = _ret_mixer(xs, w_in_ret[o], gn_g[o], gn_b[o], w_out_ret[o], state_ret[o], past_len)
            ret_p.append(sp)
            ret_s.append(ss)
        xp = _layer_norm(ALPHA * xp + yp, ln1_g[l], ln1_b[l])
        xs = _layer_norm(ALPHA * xs + ys, ln1_g[l], ln1_b[l])
        c0 = jnp.zeros((bp, CONV_W - 1, D_FF), xp.dtype)
        fp, cp = _conv_ffn(xp, w_up[l], conv_w[l], conv_b[l], w_down[l], c0)
        fs, cs = _conv_ffn(xs, w_up[l], conv_w[l], conv_b[l], w_down[l], state_ffn_conv[l])
        conv_p.append(cp)
        conv_s.append(cs)
        xp = _layer_norm(ALPHA * xp + fp, ln2_g[l], ln2_b[l])
        xs = _layer_norm(ALPHA * xs + fs, ln2_g[l], ln2_b[l])
    return (xp, xs, jnp.stack(sb_k_p), jnp.stack(sb_v_p), jnp.stack(sb_k_s), jnp.stack(sb_v_s),
            jnp.stack(chunk_v_s), jnp.stack(ret_p), jnp.stack(ret_s), jnp.stack(conv_p), jnp.stack(conv_s))
```

```python
import functools
import math

import jax
import jax.numpy as jnp
from jax import lax
from jax.experimental import pallas as pl
from jax.experimental.pallas import tpu as pltpu

F32 = jnp.float32
BF16 = jnp.bfloat16

D_MODEL = 1024
W_A = D_MODEL // 2
G_A = 8
DG_A = W_A // G_A
CHUNK = 128
W_B = D_MODEL // 2
H_B = 8
DH_B = W_B // H_B
PAGE_SIZE = 128
H_C = 4
DK_C = D_MODEL // H_C
DV_C = 2 * DK_C
RET_CHUNK = 128
ROPE_BASE = 10000.0
D_FF = ((8 * D_MODEL // 3 + 127) // 128) * 128
CONV_W = 3
DEPTH = 2
ALPHA = (2 * DEPTH) ** 0.25
LN_EPS = 1e-5

LANES = 128
FF_CHUNK = 256
VMEM_LIMIT = 56 * 1024 * 1024

_LOG_G = tuple(math.log1p(-(2.0 ** (-5.0 - h))) for h in range(H_C))


def _cparams(n_axes):
    return pltpu.CompilerParams(
        dimension_semantics=("arbitrary",) * n_axes, vmem_limit_bytes=VMEM_LIMIT)


def _const_spec(shape):
    nd = len(shape)
    return pl.BlockSpec(shape, lambda *_: (0,) * nd, pipeline_mode=pl.Buffered(1))


def _ln(x, g, b):
    mu = jnp.mean(x, axis=-1, keepdims=True)
    xc = x - mu
    var = jnp.mean(xc * xc, axis=-1, keepdims=True)
    return xc * lax.rsqrt(var + LN_EPS) * g + b


def _softplus(z):
    return jnp.maximum(z, 0.0) + jnp.log1p(jnp.exp(-jnp.abs(z)))


def _split_bf16(x):
    hi = x.astype(BF16)
    lo = (x - hi.astype(F32)).astype(BF16)
    return jnp.concatenate([hi, lo], axis=1)


def _neg_suffix_matrix(n):
    j = lax.broadcasted_iota(jnp.int32, (2 * n, n), 0)
    s = lax.broadcasted_iota(jnp.int32, (2 * n, n), 1)
    jj = jnp.where(j >= n, j - n, j)
    return jnp.where(jj > s, -1.0, 0.0).astype(BF16)


def _ab_proj_kernel(x_ref, w_ref, vg_ref, vb_ref, wmix_ref, mixb_ref, *refs, mix_len, head_major):
    if head_major:
        a_ref, q_ref, k_ref, v_ref, kf_ref, vf_ref, mixed_ref = refs
    else:
        a_ref, q_ref, kf_ref, vf_ref, va_ref, mixed_ref = refs
    tm = x_ref.shape[0]
    x = x_ref[...].astype(BF16)
    h = jnp.dot(x, w_ref[...], preferred_element_type=F32)
    u = jax.nn.gelu(h[:, :W_A])
    va = _ln(jax.nn.gelu(h[:, W_A:2 * W_A]), vg_ref[...], vb_ref[...])
    q = h[:, 2 * W_A:2 * W_A + W_B] * (DH_B ** -0.5)
    k = h[:, 2 * W_A + W_B:2 * W_A + 2 * W_B]
    v = h[:, 2 * W_A + 2 * W_B:]
    kf_ref[...] = k
    vf_ref[...] = v
    if head_major:
        for p in range(W_B // LANES):
            sl = slice(p * LANES, (p + 1) * LANES)
            q_ref[0, p] = q[:, sl].astype(BF16)
            k_ref[0, p] = k[:, sl].astype(BF16)
            v_ref[0, p] = v[:, sl].astype(BF16)
    else:
        q_ref[...] = q
        va_ref[...] = va

    r = lax.broadcasted_iota(jnp.int32, (CHUNK, CHUNK), 0)
    c = lax.broadcasted_iota(jnp.int32, (CHUNK, CHUNK), 1)
    if mix_len == CHUNK:
        mask = c <= r
    else:
        mask = jnp.logical_and(r // mix_len == c // mix_len, c % mix_len <= r % mix_len)
    wg = [jnp.where(mask, wmix_ref[g], 0.0).astype(BF16) for g in range(G_A)]
    for ci in range(tm // CHUNK):
        rows = slice(ci * CHUNK, (ci + 1) * CHUNK)
        va_c = va[rows]
        for g in range(G_A):
            cols = slice(g * DG_A, (g + 1) * DG_A)
            mixed_ref[:, cols] = jnp.dot(wg[g], va_c[:, cols].astype(BF16), preferred_element_type=F32)
        a_ref[rows, :] = (u[rows] * (mixed_ref[...] + mixb_ref[...])).astype(BF16)


def _ab_proj(x2d, w_bf, vg, vb, wmix, mixb, *, mix_len, head_major, tm, seq_len=None):
    rows = x2d.shape[0]
    d_in = w_bf.shape[1]
    n_tiles = rows // tm
    row_spec = lambda width: pl.BlockSpec((tm, width), lambda i: (i, 0))
    in_specs = [row_spec(D_MODEL), _const_spec((D_MODEL, d_in)), _const_spec((1, W_A)),
                _const_spec((1, W_A)), _const_spec((G_A, CHUNK, CHUNK)), _const_spec((CHUNK, W_A))]
    if head_major:
        tps = seq_len // tm
        batch = rows // seq_len
        hm_spec = pl.BlockSpec((1, W_B // LANES, tm, LANES), lambda i: (i // tps, 0, i % tps, 0))
        hm_shape = jax.ShapeDtypeStruct((batch, W_B // LANES, seq_len, LANES), BF16)
        out_specs = [row_spec(W_A), hm_spec, hm_spec, hm_spec, row_spec(W_B), row_spec(W_B)]
        out_shape = [jax.ShapeDtypeStruct((rows, W_A), BF16), hm_shape, hm_shape, hm_shape,
                     jax.ShapeDtypeStruct((rows, W_B), F32), jax.ShapeDtypeStruct((rows, W_B), F32)]
    else:
        out_specs = [row_spec(W_A), row_spec(W_B), row_spec(W_B), row_spec(W_B), row_spec(W_A)]
        out_shape = [jax.ShapeDtypeStruct((rows, W_A), BF16)] + [
            jax.ShapeDtypeStruct((rows, W_B), F32)] * 3 + [jax.ShapeDtypeStruct((rows, W_A), F32)]
    return pl.pallas_call(
        functools.partial(_ab_proj_kernel, mix_len=mix_len, head_major=head_major),
        grid=(n_tiles,), in_specs=in_specs, out_specs=out_specs, out_shape=out_shape,
        scratch_shapes=[pltpu.VMEM((CHUNK, W_A), F32)],
        compiler_params=_cparams(1),
        name="ab_proj_prompt" if head_major else "ab_proj_sample",
    )(x2d, w_bf, vg, vb, wmix, mixb)


def _sb_attn_kernel(bias_ref, q_ref, k_ref, v_ref, o_ref, acc_ref, carry_ref, *, blk):
    hp = pl.program_id(1)
    qi = pl.program_id(2)
    lane = lax.broadcasted_iota(jnp.int32, (blk, LANES), 1)
    q2 = q_ref[0, 0]
    qh = (jnp.where(lane < DH_B, q2, jnp.zeros_like(q2)),
          jnp.where(lane >= DH_B, q2, jnp.zeros_like(q2)))
    bias = (bias_ref[2 * hp], bias_ref[2 * hp + 1])
    neg_u = _neg_suffix_matrix(blk)
    row = lax.broadcasted_iota(jnp.int32, (blk, blk), 0)
    col = lax.broadcasted_iota(jnp.int32, (blk, blk), 1)
    causal = col < row

    def block(kb, diagonal):
        start = pl.multiple_of(kb * blk, blk)
        k2 = k_ref[0, 0, pl.ds(start, blk), :]
        v2 = v_ref[0, 0, pl.ds(start, blk), :]
        for hh in range(2):
            z = lax.dot_general(qh[hh], k2, (((1,), (1,)), ((), ())),
                                preferred_element_type=F32) + bias[hh]
            sp = _softplus(z)
            if diagonal:
                sp = jnp.where(causal, sp, 0.0)
            suf = jnp.dot(_split_bf16(sp), neg_u, preferred_element_type=F32)
            carry = carry_ref[hh]
            a = jnp.exp(z - sp + suf + jnp.concatenate([carry] * (blk // LANES), axis=1))
            if diagonal:
                a = jnp.where(causal, a, 0.0)
            pv = jnp.dot(a.astype(BF16), v2, preferred_element_type=F32)
            tot = jnp.sum(sp, axis=1, keepdims=True)
            if diagonal:
                acc_ref[hh] = pv
                carry_ref[hh] = jnp.broadcast_to(-tot, (blk, LANES))
            else:
                acc_ref[hh] += pv
                carry_ref[hh] = carry - tot

    carry_ref[...] = jnp.zeros_like(carry_ref)
    block(qi, True)

    def body(j, _):
        block(qi - 1 - j, False)
        return 0

    lax.fori_loop(0, qi, body, 0)
    o_ref[0] = jnp.where(lane < DH_B, acc_ref[0], acc_ref[1]).astype(o_ref.dtype)


def _sb_attn_prompt(q_hm, k_hm, v_hm, sb_bias, *, blk):
    batch, n_pairs, seq_len, _ = q_hm.shape
    nq = seq_len // blk
    grid_spec = pltpu.PrefetchScalarGridSpec(
        num_scalar_prefetch=1, grid=(batch, n_pairs, nq),
        in_specs=[pl.BlockSpec((1, 1, blk, LANES), lambda b, p, i, s: (b, p, i, 0)),
                  pl.BlockSpec((1, 1, seq_len, LANES), lambda b, p, i, s: (b, p, 0, 0)),
                  pl.BlockSpec((1, 1, seq_len, LANES), lambda b, p, i, s: (b, p, 0, 0))],
        out_specs=pl.BlockSpec((1, blk, LANES), lambda b, p, i, s: (b, i, p)),
        scratch_shapes=[pltpu.VMEM((2, blk, LANES), F32), pltpu.VMEM((2, blk, LANES), F32)])
    return pl.pallas_call(
        functools.partial(_sb_attn_kernel, blk=blk), grid_spec=grid_spec,
        out_shape=jax.ShapeDtypeStruct((batch, seq_len, n_pairs * LANES), BF16),
        compiler_params=_cparams(3), name="sb_attn_prompt",
    )(sb_bias, q_hm, k_hm, v_hm)


def _sb_decode_kernel(pt_ref, qbd_ref, bias_ref, knew_ref, vnew_ref, *refs, pages_per_step, n_q):
    del pt_ref
    k_refs = refs[:pages_per_step]
    v_refs = refs[pages_per_step:2 * pages_per_step]
    o_ref, acc_ref, carry_ref = refs[2 * pages_per_step:]
    j = pl.program_id(1)
    n_rows = qbd_ref.shape[1]
    qbd = qbd_ref[0]
    bias = bias_ref[...]
    neg_u = _neg_suffix_matrix(PAGE_SIZE)

    def page(k_pg, v_pg, valid):
        z = jnp.dot(qbd, k_pg.astype(BF16), preferred_element_type=F32) + bias
        sp = _softplus(z)
        if valid is not None:
            sp = jnp.where(valid, sp, 0.0)
        suf = jnp.dot(_split_bf16(sp), neg_u, preferred_element_type=F32)
        carry = carry_ref[...]
        a = jnp.exp(z - sp + suf + carry)
        if valid is not None:
            a = jnp.where(valid, a, 0.0)
        acc_ref[...] += lax.dot_general(a.astype(BF16), v_pg.astype(BF16), (((1,), (1,)), ((), ())),
                                        preferred_element_type=F32)
        carry_ref[...] = carry - jnp.sum(sp, axis=1, keepdims=True)

    @pl.when(j == 0)
    def _():
        acc_ref[...] = jnp.zeros_like(acc_ref)
        carry_ref[...] = jnp.zeros_like(carry_ref)
        t = lax.broadcasted_iota(jnp.int32, (n_rows, PAGE_SIZE), 0) // H_B
        i = lax.broadcasted_iota(jnp.int32, (n_rows, PAGE_SIZE), 1)
        page(knew_ref[0], vnew_ref[0], i < t)

    for p in range(pages_per_step):
        page(k_refs[p][0], v_refs[p][0], None)

    @pl.when(j == pl.num_programs(1) - 1)
    def _():
        rr = lax.broadcasted_iota(jnp.int32, (n_rows, W_B), 0) % H_B
        cc = lax.broadcasted_iota(jnp.int32, (n_rows, W_B), 1) // DH_B
        own = jnp.where(rr == cc, acc_ref[...], 0.0)
        o_ref[0] = jnp.sum(own.reshape(n_q, H_B, W_B), axis=1)


def _sb_decode(page_table, qbd, bias_tile, k_new, v_new, pool_k, pool_v, *, pages_per_step):
    n_seq, n_pages = page_table.shape
    n_rows = qbd.shape[1]
    n_q = n_rows // H_B
    steps = n_pages // pages_per_step

    def page_spec(p):
        return pl.BlockSpec(
            (1, W_B, PAGE_SIZE),
            lambda b, j, pt, p=p: (pt[b, n_pages - 1 - (j * pages_per_step + p)], 0, 0))

    grid_spec = pltpu.PrefetchScalarGridSpec(
        num_scalar_prefetch=1, grid=(n_seq, steps),
        in_specs=[pl.BlockSpec((1, n_rows, W_B), lambda b, j, pt: (b, 0, 0)),
                  pl.BlockSpec((n_rows, LANES), lambda b, j, pt: (0, 0)),
                  pl.BlockSpec((1, W_B, PAGE_SIZE), lambda b, j, pt: (b, 0, 0)),
                  pl.BlockSpec((1, W_B, PAGE_SIZE), lambda b, j, pt: (b, 0, 0))]
                 + [page_spec(p) for p in range(pages_per_step)] * 2,
        out_specs=pl.BlockSpec((1, n_q, W_B), lambda b, j, pt: (b, 0, 0)),
        scratch_shapes=[pltpu.VMEM((n_rows, W_B), F32), pltpu.VMEM((n_rows, LANES), F32)])
    return pl.pallas_call(
        functools.partial(_sb_decode_kernel, pages_per_step=pages_per_step, n_q=n_q),
        grid_spec=grid_spec, out_shape=jax.ShapeDtypeStruct((n_seq, n_q, W_B), F32),
        compiler_params=_cparams(2), name="sb_decode",
    )(page_table, qbd, bias_tile, k_new, v_new, *([pool_k] * pages_per_step), *([pool_v] * pages_per_step))


def _out_ln_kernel(x_ref, a1_ref, a2_ref, w_ref, g_ref, b_ref, y_ref):
    a = jnp.concatenate([a1_ref[...], a2_ref[...]], axis=1)
    y = jnp.dot(a, w_ref[...], preferred_element_type=F32)
    y_ref[...] = _ln(ALPHA * x_ref[...] + y, g_ref[...], b_ref[...])


def _out_ln(x2d, a1, a2, w_bf, g, b, *, tm, a2_block_col=0, name):
    rows = x2d.shape[0]
    k1 = w_bf.shape[0] // 2
    return pl.pallas_call(
        _out_ln_kernel, grid=(rows // tm,),
        in_specs=[pl.BlockSpec((tm, D_MODEL), lambda i: (i, 0)),
                  pl.BlockSpec((tm, k1), lambda i: (i, 0)),
                  pl.BlockSpec((tm, k1), lambda i: (i, a2_block_col)),
                  _const_spec(w_bf.shape), _const_spec((1, D_MODEL)), _const_spec((1, D_MODEL))],
        out_specs=pl.BlockSpec((tm, D_MODEL), lambda i: (i, 0)),
        out_shape=jax.ShapeDtypeStruct((rows, D_MODEL), F32),
        compiler_params=_cparams(1), name=name,
    )(x2d, a1, a2, w_bf, g, b)


def _ffn_chunk(x_bf, wup_ref, cw_ref, cb_ref, ci, shift_fix):
    cols = slice(ci * FF_CHUNK, (ci + 1) * FF_CHUNK)
    vcols = slice(D_FF + ci * FF_CHUNK, D_FF + (ci + 1) * FF_CHUNK)
    g = jnp.dot(x_bf, wup_ref[:, cols], preferred_element_type=F32)
    val = jnp.dot(x_bf, wup_ref[:, vcols], preferred_element_type=F32)
    g1, g2 = shift_fix(g, cols)
    cw = cw_ref[:, cols]
    conv = cb_ref[:, cols] + cw[0:1] * g2 + cw[1:2] * g1 + cw[2:3] * g
    return g, (jax.nn.gelu(conv) * val).astype(BF16)


def _ffn_prompt_kernel(x_ref, wup_ref, cw_ref, cb_ref, wdn_ref, g_ref, b_ref,
                       y_ref, conv_ref, act_ref, prev_ref):
    tm = x_ref.shape[0]

    @pl.when(pl.program_id(1) == 0)
    def _():
        prev_ref[...] = jnp.zeros_like(prev_ref)

    x = x_ref[...]
    x_bf = x.astype(BF16)
    row = lax.broadcasted_iota(jnp.int32, (tm, FF_CHUNK), 0)

    def shift_fix(g, cols):
        p0 = prev_ref[0:1, cols]
        p1 = prev_ref[1:2, cols]
        g1 = jnp.where(row == 0, p1, pltpu.roll(g, 1, 0))
        g2 = jnp.where(row == 0, p0, jnp.where(row == 1, p1, pltpu.roll(g, 2, 0)))
        return g1, g2

    for ci in range(D_FF // FF_CHUNK):
        cols = slice(ci * FF_CHUNK, (ci + 1) * FF_CHUNK)
        g, act = _ffn_chunk(x_bf, wup_ref, cw_ref, cb_ref, ci, shift_fix)
        act_ref[:, cols] = act
        prev_ref[0:2, cols] = g[tm - 2:tm]
        conv_ref[0, :, cols] = g[tm - 2:tm]
    y = jnp.dot(act_ref[...], wdn_ref[...], preferred_element_type=F32)
    y_ref[...] = _ln(ALPHA * x + y, g_ref[...], b_ref[...])


def _ffn_sample_kernel(x_ref, wup_ref, cw_ref, cb_ref, wdn_ref, g_ref, b_ref, inj1_ref, inj2_ref,
                       y_ref, gate_ref, act_ref, *, seq_len):
    tm = x_ref.shape[0]
    x = x_ref[...]
    x_bf = x.astype(BF16)
    pos = lax.broadcasted_iota(jnp.int32, (tm, FF_CHUNK), 0) % seq_len

    def shift_fix(g, cols):
        g1 = jnp.where(pos >= 1, pltpu.roll(g, 1, 0), inj1_ref[:, cols])
        g2 = jnp.where(pos >= 2, pltpu.roll(g, 2, 0), inj2_ref[:, cols])
        return g1, g2

    for ci in range(D_FF // FF_CHUNK):
        cols = slice(ci * FF_CHUNK, (ci + 1) * FF_CHUNK)
        g, act = _ffn_chunk(x_bf, wup_ref, cw_ref, cb_ref, ci, shift_fix)
        act_ref[:, cols] = act
        gate_ref[:, cols] = g
    y = jnp.dot(act_ref[...], wdn_ref[...], preferred_element_type=F32)
    y_ref[...] = _ln(ALPHA * x + y, g_ref[...], b_ref[...])


def _ffn_weight_specs():
    return [_const_spec((D_MODEL, 2 * D_FF)), _const_spec((CONV_W, D_FF)), _const_spec((1, D_FF)),
            _const_spec((D_FF, D_MODEL)), _const_spec((1, D_MODEL)), _const_spec((1, D_MODEL))]


def _ffn_prompt(x2d, wup_bf, cw, cb, wdn_bf, g, b, *, tm, seq_len):
    rows = x2d.shape[0]
    batch = rows // seq_len
    tps = seq_len // tm
    return pl.pallas_call(
        _ffn_prompt_kernel, grid=(batch, tps),
        in_specs=[pl.BlockSpec((tm, D_MODEL), lambda bi, t: (bi * tps + t, 0))] + _ffn_weight_specs(),
        out_specs=[pl.BlockSpec((tm, D_MODEL), lambda bi, t: (bi * tps + t, 0)),
                   pl.BlockSpec((1, CONV_W - 1, D_FF), lambda bi, t: (bi, 0, 0))],
        out_shape=[jax.ShapeDtypeStruct((rows, D_MODEL), F32),
                   jax.ShapeDtypeStruct((batch, CONV_W - 1, D_FF), F32)],
        scratch_shapes=[pltpu.VMEM((tm, D_FF), BF16), pltpu.VMEM((8, D_FF), F32)],
        compiler_params=_cparams(2), name="ffn_prompt",
    )(x2d, wup_bf, cw, cb, wdn_bf, g, b)


def _ffn_sample(x2d, wup_bf, cw, cb, wdn_bf, g, b, conv_state, *, seq_len):
    rows = x2d.shape[0]
    n_seq = rows // seq_len
    zeros = jnp.zeros((n_seq, 1, D_FF), F32)
    inj1 = jnp.concatenate([conv_state[:, 1:2]] + [zeros] * (seq_len - 1), axis=1).reshape(rows, D_FF)
    inj2 = jnp.concatenate([conv_state] + [zeros] * (seq_len - 2), axis=1).reshape(rows, D_FF)
    full = lambda width: pl.BlockSpec((rows, width), lambda i: (0, 0))
    y, gate = pl.pallas_call(
        functools.partial(_ffn_sample_kernel, seq_len=seq_len), grid=(1,),
        in_specs=[full(D_MODEL)] + _ffn_weight_specs() + [full(D_FF), full(D_FF)],
        out_specs=[full(D_MODEL), full(D_FF)],
        out_shape=[jax.ShapeDtypeStruct((rows, D_MODEL), F32), jax.ShapeDtypeStruct((rows, D_FF), F32)],
        scratch_shapes=[pltpu.VMEM((rows, D_FF), BF16)],
        compiler_params=_cparams(1), name="ffn_sample",
    )(x2d, wup_bf, cw, cb, wdn_bf, g, b, inj1, inj2)
    new_state = gate.reshape(n_seq, seq_len, D_FF)[:, seq_len - (CONV_W - 1):]
    return y, new_state


def _ret_kernel(x_ref, w_ref, cos_ref, sin_ref, gg_ref, gb_ref, *refs, decay_len, has_state):
    if has_state:
        s0_ref, y_ref, sout_ref, s_ref = refs
    else:
        y_ref, sout_ref, s_ref = refs
    tm = x_ref.shape[0]
    t = pl.program_id(1)

    @pl.when(t == 0)
    def _():
        if has_state:
            s_ref[...] = s0_ref[0]
        else:
            s_ref[...] = jnp.zeros_like(s_ref)

    x = x_ref[...].astype(BF16)
    cos = cos_ref[...]
    sin = sin_ref[...]
    half = DK_C // 2

    def rotary(a):
        a1, a2 = a[:, :half], a[:, half:]
        return jnp.concatenate([a1 * cos - a2 * sin, a1 * sin + a2 * cos], axis=1)

    ii = lax.broadcasted_iota(jnp.int32, (RET_CHUNK, RET_CHUNK), 0)
    jj = lax.broadcasted_iota(jnp.int32, (RET_CHUNK, RET_CHUNK), 1)
    diff = jnp.where(ii >= jj, ii - jj, 0).astype(F32)
    idx = lax.broadcasted_iota(jnp.int32, (RET_CHUNK, 1), 0).astype(F32)
    v_off = 2 * H_C * DK_C
    g_off = v_off + H_C * DV_C
    for h in range(H_C):
        log_g = _LOG_G[h]
        dmat = jnp.where(ii >= jj, jnp.exp(diff * log_g), 0.0)
        q_dec = jnp.exp((idx + 1.0) * log_g)
        k_dec = jnp.exp((decay_len - 1.0 - idx) * log_g)
        c_dec = math.exp(decay_len * log_g)
        q = rotary(jnp.dot(x, w_ref[:, h * DK_C:(h + 1) * DK_C], preferred_element_type=F32))
        k = rotary(jnp.dot(x, w_ref[:, (H_C + h) * DK_C:(H_C + h + 1) * DK_C],
                           preferred_element_type=F32)) * (DK_C ** -0.5)
        v = jnp.dot(x, w_ref[:, v_off + h * DV_C:v_off + (h + 1) * DV_C], preferred_element_type=F32)
        gate = jnp.dot(x, w_ref[:, g_off + h * DV_C:g_off + (h + 1) * DV_C], preferred_element_type=F32)
        vcols = slice(h * DV_C, (h + 1) * DV_C)
        for ci in range(tm // RET_CHUNK):
            rows = slice(ci * RET_CHUNK, (ci + 1) * RET_CHUNK)
            qc = q[rows].astype(BF16)
            kc = k[rows]
            vc = v[rows].astype(BF16)
            state = s_ref[h]
            sc = lax.dot_general(qc, kc.astype(BF16), (((1,), (1,)), ((), ())),
                                 preferred_element_type=F32) * dmat
            o = (jnp.dot(sc.astype(BF16), vc, preferred_element_type=F32)
                 + jnp.dot(qc, state.astype(BF16), preferred_element_type=F32) * q_dec)
            kd = (kc * k_dec).astype(BF16)
            s_ref[h] = state * c_dec + lax.dot_general(
                kd, vc, (((0,), (0,)), ((), ())), preferred_element_type=F32)
            mu = jnp.mean(o, axis=-1, keepdims=True)
            oc = o - mu
            var = jnp.mean(oc * oc, axis=-1, keepdims=True)
            normed = oc * lax.rsqrt(var + LN_EPS) * gg_ref[:, vcols] + gb_ref[:, vcols]
            y_ref[rows, vcols] = (jax.nn.silu(gate[rows]) * normed).astype(y_ref.dtype)

    @pl.when(t == pl.num_programs(1) - 1)
    def _():
        sout_ref[0] = s_ref[...]


def _retention(x2d, w_bf, cos, sin, gg, gb, state0, *, tm, seq_len, decay_len, name):
    rows = x2d.shape[0]
    n_seq = rows // seq_len
    tps = seq_len // tm
    d_y = H_C * DV_C
    has_state = state0 is not None
    state_spec = pl.BlockSpec((1, H_C, DK_C, DV_C), lambda bi, t: (bi, 0, 0, 0))
    in_specs = [pl.BlockSpec((tm, D_MODEL), lambda bi, t: (bi * tps + t, 0)),
                _const_spec(w_bf.shape),
                pl.BlockSpec((tm, DK_C // 2), lambda bi, t: (t, 0)),
                pl.BlockSpec((tm, DK_C // 2), lambda bi, t: (t, 0)),
                _const_spec((1, d_y)), _const_spec((1, d_y))]
    args = [x2d, w_bf, cos, sin, gg, gb]
    if has_state:
        in_specs.append(state_spec)
        args.append(state0)
    return pl.pallas_call(
        functools.partial(_ret_kernel, decay_len=decay_len, has_state=has_state),
        grid=(n_seq, tps), in_specs=in_specs,
        out_specs=[pl.BlockSpec((tm, d_y), lambda bi, t: (bi * tps + t, 0)), state_spec],
        out_shape=[jax.ShapeDtypeStruct((rows, d_y), BF16),
                   jax.ShapeDtypeStruct((n_seq, H_C, DK_C, DV_C), F32)],
        scratch_shapes=[pltpu.VMEM((H_C, DK_C, DV_C), F32)],
        compiler_params=_cparams(2), name=name,
    )(*args)


def _rope_tables(pos):
    half = DK_C // 2
    inv = ROPE_BASE ** (-jnp.arange(half, dtype=F32) / half)
    ang = pos.astype(F32)[:, None] * inv[None, :]
    return jnp.cos(ang), jnp.sin(ang)


def kernel(x_prompt, x_sample, cache_sb_k, cache_sb_v, state_ret, state_ffn_conv, page_table,
           w_in_ab, vln_g, vln_b, w_s, b_s, sb_bias, w_out_ab, w_in_ret, gn_g, gn_b, w_out_ret,
           ln1_g, ln1_b, ln2_g, ln2_b, w_up, conv_w, conv_b, w_down):
    bp, tp, _ = x_prompt.shape
    bs, ts, _ = x_sample.shape
    n_pages = page_table.shape[1]
    past_len = n_pages * PAGE_SIZE
    n_phys = cache_sb_k.shape[1]
    xp = x_prompt.reshape(bp * tp, D_MODEL)
    xs = x_sample.reshape(bs * ts, D_MODEL)
    row2d = lambda a: a.reshape(1, -1)

    sb_k_p, sb_v_p, sb_k_s, sb_v_s, chunk_v_s = [], [], [], [], []
    ret_p, ret_s, conv_p, conv_s = [], [], [], []
    for l in range(DEPTH):
        if l % 2 == 0:
            e = l // 2
            w_in = w_in_ab[e].astype(BF16)
            vg, vb = row2d(vln_g[e]), row2d(vln_b[e])
            mixb_p = jnp.repeat(b_s[e].T, DG_A, axis=1)
            a_p, q_hm, k_hm, v_hm, kf_p, vf_p = _ab_proj(
                xp, w_in, vg, vb, w_s[e], mixb_p, mix_len=CHUNK, head_major=True, tm=512, seq_len=tp)
            b_p = _sb_attn_prompt(q_hm, k_hm, v_hm, sb_bias[e], blk=256)

            reps = CHUNK // ts
            wmix_s = jnp.tile(w_s[e][:, :ts, :ts], (1, reps, reps))
            mixb_s = jnp.repeat(jnp.tile(b_s[e][:, :ts], (1, reps)).T, DG_A, axis=1)
            a_s, q_s, kf_s, vf_s, va_s = _ab_proj(
                xs, w_in, vg, vb, wmix_s, mixb_s, mix_len=ts, head_major=False, tm=bs * ts)
            head_of_col = jnp.arange(W_B) // DH_B
            own = (jnp.arange(H_B)[:, None] == head_of_col[None, :]).astype(F32)
            qbd = (q_s.reshape(bs, ts, 1, W_B) * own[None, None]).reshape(bs, ts * H_B, W_B).astype(BF16)
            bias_tile = jnp.broadcast_to(jnp.tile(sb_bias[e], ts)[:, None], (ts * H_B, LANES))
            pad = ((0, 0), (0, 0), (0, PAGE_SIZE - ts))
            k_new = jnp.pad(kf_s.reshape(bs, ts, W_B).transpose(0, 2, 1), pad)
            v_new = jnp.pad(vf_s.reshape(bs, ts, W_B).transpose(0, 2, 1), pad)
            pool_k = cache_sb_k[e].transpose(0, 2, 3, 1).reshape(n_phys, W_B, PAGE_SIZE)
            pool_v = cache_sb_v[e].transpose(0, 2, 3, 1).reshape(n_phys, W_B, PAGE_SIZE)
            b_s_out = _sb_decode(page_table, qbd, bias_tile, k_new, v_new, pool_k, pool_v,
                                 pages_per_step=8)

            w_out = w_out_ab[e].astype(BF16)
            g1, b1 = row2d(ln1_g[l]), row2d(ln1_b[l])
            xp = _out_ln(xp, a_p, b_p.reshape(bp * tp, W_B), w_out, g1, b1, tm=512, name="out_ln_ab_prompt")
            xs = _out_ln(xs, a_s, b_s_out.reshape(bs * ts, W_B).astype(BF16), w_out, g1, b1,
                         tm=bs * ts, name="out_ln_ab_sample")
            sb_k_p.append(kf_p.reshape(bp, tp // PAGE_SIZE, PAGE_SIZE, H_B, DH_B))
            sb_v_p.append(vf_p.reshape(bp, tp // PAGE_SIZE, PAGE_SIZE, H_B, DH_B))
            sb_k_s.append(kf_s.reshape(bs, ts, H_B, DH_B))
            sb_v_s.append(vf_s.reshape(bs, ts, H_B, DH_B))
            chunk_v_s.append(va_s.reshape(bs, ts, G_A, DG_A))
        else:
            o = l // 2
            w_in = w_in_ret[o].astype(BF16)
            gg, gb = row2d(gn_g[o]), row2d(gn_b[o])
            cos_p, sin_p = _rope_tables(jnp.arange(tp))
            y_p, s_p = _retention(xp, w_in, cos_p, sin_p, gg, gb, None, tm=256, seq_len=tp,
                                  decay_len=RET_CHUNK, name="retention_prompt")
            cos_s, sin_s = _rope_tables(past_len + jnp.arange(RET_CHUNK))
            xs_pad = jnp.pad(xs.reshape(bs, ts, D_MODEL), ((0, 0), (0, RET_CHUNK - ts), (0, 0)))
            y_s, s_s = _retention(xs_pad.reshape(bs * RET_CHUNK, D_MODEL), w_in, cos_s, sin_s, gg, gb,
                                  state_ret[o], tm=RET_CHUNK, seq_len=RET_CHUNK, decay_len=ts,
                                  name="retention_sample")
            y_s = y_s.reshape(bs, RET_CHUNK, H_C * DV_C)[:, :ts].reshape(bs * ts, H_C * DV_C)
            w_out = w_out_ret[o].astype(BF16)
            g1, b1 = row2d(ln1_g[l]), row2d(ln1_b[l])
            xp = _out_ln(xp, y_p, y_p, w_out, g1, b1, tm=512, a2_block_col=1, name="out_ln_ret_prompt")
            xs = _out_ln(xs, y_s, y_s, w_out, g1, b1, tm=bs * ts, a2_block_col=1, name="out_ln_ret_sample")
            ret_p.append(s_p)
            ret_s.append(s_s)

        wup = w_up[l].astype(BF16)
        wdn = w_down[l].astype(BF16)
        g2, b2 = row2d(ln2_g[l]), row2d(ln2_b[l])
        xp, cp = _ffn_prompt(xp, wup, conv_w[l], row2d(conv_b[l]), wdn, g2, b2, tm=512, seq_len=tp)
        xs, cs = _ffn_sample(xs, wup, conv_w[l], row2d(conv_b[l]), wdn, g2, b2, state_ffn_conv[l], seq_len=ts)
        conv_p.append(cp)
        conv_s.append(cs)

    return (xp.reshape(bp, tp, D_MODEL), xs.reshape(bs, ts, D_MODEL),
            jnp.stack(sb_k_p), jnp.stack(sb_v_p), jnp.stack(sb_k_s), jnp.stack(sb_v_s),
            jnp.stack(chunk_v_s), jnp.stack(ret_p), jnp.stack(ret_s), jnp.stack(conv_p), jnp.stack(conv_s))
```

```python
import functools
import math

import jax
import jax.numpy as jnp
from jax import lax
from jax.experimental import pallas as pl
from jax.experimental.pallas import tpu as pltpu

F32 = jnp.float32
BF16 = jnp.bfloat16

D_MODEL = 1024
W_A = D_MODEL // 2
G_A = 8
DG_A = W_A // G_A
CHUNK = 128
W_B = D_MODEL // 2
H_B = 8
DH_B = W_B // H_B
PAGE_SIZE = 128
H_C = 4
DK_C = D_MODEL // H_C
DV_C = 2 * DK_C
RET_CHUNK = 128
ROPE_BASE = 10000.0
D_FF = ((8 * D_MODEL // 3 + 127) // 128) * 128
CONV_W = 3
DEPTH = 2
ALPHA = (2 * DEPTH) ** 0.25
LN_EPS = 1e-5
LOG2E = 1.4426950408889634
EXP2_CLAMP = 126.0
BIAS_PIECES = 3

LANES = 128
FF_CHUNK = 256
VMEM_LIMIT = 56 * 1024 * 1024

_LOG_G = tuple(math.log1p(-(2.0 ** (-5.0 - h))) for h in range(H_C))


def _cparams(n_axes):
    return pltpu.CompilerParams(
        dimension_semantics=("arbitrary",) * n_axes, vmem_limit_bytes=VMEM_LIMIT)


def _const_spec(shape):
    nd = len(shape)
    return pl.BlockSpec(shape, lambda *_: (0,) * nd, pipeline_mode=pl.Buffered(1))


def _ln(x, g, b):
    mu = jnp.mean(x, axis=-1, keepdims=True)
    xc = x - mu
    var = jnp.mean(xc * xc, axis=-1, keepdims=True)
    return xc * lax.rsqrt(var + LN_EPS) * g + b


def _softplus2(w):
    return jnp.maximum(w, jnp.log2(1.0 + jnp.exp2(jnp.minimum(w, EXP2_CLAMP))))


def _split_bf16(x):
    hi = x.astype(BF16)
    lo = (x - hi.astype(F32)).astype(BF16)
    return jnp.concatenate([hi, lo], axis=1)


def _neg_suffix_matrix(n):
    j = lax.broadcasted_iota(jnp.int32, (2 * n, n), 0)
    s = lax.broadcasted_iota(jnp.int32, (2 * n, n), 1)
    jj = jnp.where(j >= n, j - n, j)
    return jnp.where(jj > s, -1.0, 0.0).astype(BF16)


def _ab_proj_kernel(x_ref, w_ref, vg_ref, vb_ref, wmix_ref, mixb_ref, *refs, mix_len, head_major):
    if head_major:
        a_ref, q_ref, k_ref, v_ref, kf_ref, vf_ref, mixed_ref = refs
    else:
        a_ref, q_ref, kf_ref, vf_ref, va_ref, mixed_ref = refs
    tm = x_ref.shape[0]
    x = x_ref[...].astype(BF16)
    h = jnp.dot(x, w_ref[...], preferred_element_type=F32)
    u = jax.nn.gelu(h[:, :W_A])
    va = _ln(jax.nn.gelu(h[:, W_A:2 * W_A]), vg_ref[...], vb_ref[...])
    q = h[:, 2 * W_A:2 * W_A + W_B] * (DH_B ** -0.5 * LOG2E)
    k = h[:, 2 * W_A + W_B:2 * W_A + 2 * W_B]
    v = h[:, 2 * W_A + 2 * W_B:]
    if head_major:
        for p in range(W_B // LANES):
            sl = slice(p * LANES, (p + 1) * LANES)
            q_ref[0, p] = q[:, sl].astype(BF16)
            k_ref[0, p] = k[:, sl].astype(BF16)
            v_ref[0, p] = v[:, sl].astype(BF16)
        for pg in range(tm // PAGE_SIZE):
            rows = slice(pg * PAGE_SIZE, (pg + 1) * PAGE_SIZE)
            kf_ref[pg] = k[rows].T
            vf_ref[pg] = v[rows].T
    else:
        kf_ref[...] = k
        vf_ref[...] = v
        q_ref[...] = q
        va_ref[...] = va

    r = lax.broadcasted_iota(jnp.int32, (CHUNK, CHUNK), 0)
    c = lax.broadcasted_iota(jnp.int32, (CHUNK, CHUNK), 1)
    if mix_len == CHUNK:
        mask = c <= r
    else:
        mask = jnp.logical_and(r // mix_len == c // mix_len, c % mix_len <= r % mix_len)
    wg = [jnp.where(mask, wmix_ref[g], 0.0).astype(BF16) for g in range(G_A)]
    for ci in range(tm // CHUNK):
        rows = slice(ci * CHUNK, (ci + 1) * CHUNK)
        va_c = va[rows]
        for g in range(G_A):
            cols = slice(g * DG_A, (g + 1) * DG_A)
            mixed_ref[:, cols] = jnp.dot(wg[g], va_c[:, cols].astype(BF16), preferred_element_type=F32)
        a_ref[rows, :] = (u[rows] * (mixed_ref[...] + mixb_ref[...])).astype(BF16)


def _ab_proj(x2d, w_bf, vg, vb, wmix, mixb, *, mix_len, head_major, tm, seq_len=None):
    rows = x2d.shape[0]
    d_in = w_bf.shape[1]
    n_tiles = rows // tm
    row_spec = lambda width: pl.BlockSpec((tm, width), lambda i: (i, 0))
    in_specs = [row_spec(D_MODEL), _const_spec((D_MODEL, d_in)), _const_spec((1, W_A)),
                _const_spec((1, W_A)), _const_spec((G_A, CHUNK, CHUNK)), _const_spec((CHUNK, W_A))]
    if head_major:
        tps = seq_len // tm
        batch = rows // seq_len
        hm_spec = pl.BlockSpec((1, W_B // LANES, tm, LANES), lambda i: (i // tps, 0, i % tps, 0))
        hm_shape = jax.ShapeDtypeStruct((batch, W_B // LANES, seq_len, LANES), BF16)
        ppt = tm // PAGE_SIZE
        pg_spec = pl.BlockSpec((ppt, W_B, PAGE_SIZE), lambda i: (i, 0, 0))
        pg_shape = jax.ShapeDtypeStruct((rows // PAGE_SIZE, W_B, PAGE_SIZE), F32)
        out_specs = [row_spec(W_A), hm_spec, hm_spec, hm_spec, pg_spec, pg_spec]
        out_shape = [jax.ShapeDtypeStruct((rows, W_A), BF16), hm_shape, hm_shape, hm_shape,
                     pg_shape, pg_shape]
    else:
        out_specs = [row_spec(W_A), row_spec(W_B), row_spec(W_B), row_spec(W_B), row_spec(W_A)]
        out_shape = [jax.ShapeDtypeStruct((rows, W_A), BF16)] + [
            jax.ShapeDtypeStruct((rows, W_B), F32)] * 3 + [jax.ShapeDtypeStruct((rows, W_A), F32)]
    return pl.pallas_call(
        functools.partial(_ab_proj_kernel, mix_len=mix_len, head_major=head_major),
        grid=(n_tiles,), in_specs=in_specs, out_specs=out_specs, out_shape=out_shape,
        scratch_shapes=[pltpu.VMEM((CHUNK, W_A), F32)],
        compiler_params=_cparams(1),
        name="ab_proj_prompt" if head_major else "ab_proj_sample",
    )(x2d, w_bf, vg, vb, wmix, mixb)


def _sb_attn_kernel(bias_ref, q_ref, k_ref, v_ref, o_ref, acc_ref, carry_ref, *, blk):
    qi = pl.program_id(1)
    n_pairs = q_ref.shape[1]
    n_heads = 2 * n_pairs
    lane = lax.broadcasted_iota(jnp.int32, (blk, LANES), 1)
    neg_u = _neg_suffix_matrix(blk)
    row = lax.broadcasted_iota(jnp.int32, (blk, blk), 0)
    col = lax.broadcasted_iota(jnp.int32, (blk, blk), 1)
    causal = col < row
    k_ones = jnp.where(lane < BIAS_PIECES, 1.0, 0.0).astype(BF16)
    q_aug = []
    for h in range(n_heads):
        q2 = q_ref[0, h // 2]
        qh = jnp.where((lane >= DH_B) == bool(h % 2), q2, jnp.zeros_like(q2))
        q_aug.append(jnp.concatenate(
            [qh, jnp.broadcast_to(bias_ref[h:h + 1, :], (blk, LANES))], axis=1))

    def blocks(kbs, diagonal):
        starts = [kb * blk if isinstance(kb, int) else pl.multiple_of(kb * blk, blk) for kb in kbs]
        n_items = len(kbs) * n_heads
        vals = [dict() for _ in range(n_items)]

        def logits(i):
            h = i % n_heads
            k2 = k_ref[0, h // 2, pl.ds(starts[i // n_heads], blk), :]
            vals[i]["w"] = lax.dot_general(q_aug[h], jnp.concatenate([k2, k_ones], axis=1),
                                           (((1,), (1,)), ((), ())), preferred_element_type=F32)

        def keep(i):
            h = i % n_heads
            w = vals[i].pop("w")
            sp = _softplus2(w)
            if diagonal:
                sp = jnp.where(causal, sp, 0.0)
                vals[i]["d"] = w - sp
                carry_ref[h] = jnp.broadcast_to(-jnp.sum(sp, axis=1, keepdims=True), (blk, LANES))
            else:
                carry = carry_ref[h]
                vals[i]["d"] = w - sp + jnp.concatenate([carry] * (blk // LANES), axis=1)
                carry_ref[h] = carry - jnp.sum(sp, axis=1, keepdims=True)
            vals[i]["split"] = _split_bf16(sp)

        def suffix(i):
            vals[i]["suf"] = jnp.dot(vals[i].pop("split"), neg_u, preferred_element_type=F32)

        def weights(i):
            a = jnp.exp2(vals[i].pop("d") + vals[i].pop("suf"))
            if diagonal:
                a = jnp.where(causal, a, 0.0)
            vals[i]["a"] = a.astype(BF16)

        def values(i):
            h = i % n_heads
            v2 = v_ref[0, h // 2, pl.ds(starts[i // n_heads], blk), :]
            pv = jnp.dot(vals[i].pop("a"), v2, preferred_element_type=F32)
            if diagonal:
                acc_ref[h] = pv
            else:
                acc_ref[h] += pv

        stages = (logits, keep, suffix, weights, values)
        for t in range(n_items + len(stages) - 1):
            for s, stage in enumerate(stages):
                if 0 <= t - s < n_items:
                    stage(t - s)

    blocks([qi], True)

    def pair(j, _):
        blocks([qi - 1 - 2 * j, qi - 2 - 2 * j], False)
        return 0

    lax.fori_loop(0, qi // 2, pair, 0)

    @pl.when(qi % 2 == 1)
    def _():
        blocks([0], False)

    for hp in range(n_pairs):
        o_ref[0, :, hp * LANES:(hp + 1) * LANES] = jnp.where(
            lane < DH_B, acc_ref[2 * hp], acc_ref[2 * hp + 1]).astype(o_ref.dtype)


def _bias_rows(bias2):
    pieces, rest = [], bias2.astype(F32)
    for _ in range(BIAS_PIECES):
        piece = rest.astype(BF16)
        pieces.append(piece)
        rest = rest - piece.astype(F32)
    rows = jnp.stack(pieces, axis=1)
    return jnp.pad(rows, ((0, 0), (0, LANES - BIAS_PIECES)))


def _sb_attn_prompt(q_hm, k_hm, v_hm, bias2, *, blk):
    batch, n_pairs, seq_len, _ = q_hm.shape
    nq = seq_len // blk
    n_heads = 2 * n_pairs
    return pl.pallas_call(
        functools.partial(_sb_attn_kernel, blk=blk), grid=(batch, nq),
        in_specs=[pl.BlockSpec((n_heads, LANES), lambda b, i: (0, 0)),
                  pl.BlockSpec((1, n_pairs, blk, LANES), lambda b, i: (b, 0, i, 0)),
                  pl.BlockSpec((1, n_pairs, seq_len, LANES), lambda b, i: (b, 0, 0, 0)),
                  pl.BlockSpec((1, n_pairs, seq_len, LANES), lambda b, i: (b, 0, 0, 0))],
        out_specs=pl.BlockSpec((1, blk, n_pairs * LANES), lambda b, i: (b, i, 0)),
        out_shape=jax.ShapeDtypeStruct((batch, seq_len, n_pairs * LANES), BF16),
        scratch_shapes=[pltpu.VMEM((n_heads, blk, LANES), F32), pltpu.VMEM((n_heads, blk, LANES), F32)],
        compiler_params=_cparams(2), name="sb_attn_prompt",
    )(_bias_rows(bias2), q_hm, k_hm, v_hm)


def _sb_decode_kernel(pt_ref, qbd_ref, bias_ref, knew_ref, vnew_ref, *refs, pages_per_step, n_q):
    del pt_ref
    k_refs = refs[:pages_per_step]
    v_refs = refs[pages_per_step:2 * pages_per_step]
    o_ref, acc_ref, carry_ref = refs[2 * pages_per_step:]
    j = pl.program_id(1)
    n_rows = qbd_ref.shape[1]
    qbd = qbd_ref[0]
    bias = bias_ref[...]
    neg_u = _neg_suffix_matrix(PAGE_SIZE)

    def pages(k_pgs, v_pgs, valid):
        n = len(k_pgs)
        k_cat = jnp.concatenate([k.astype(BF16) for k in k_pgs], axis=1)
        w_cat = jnp.dot(qbd, k_cat, preferred_element_type=F32)
        w = jnp.concatenate([w_cat[:, p * PAGE_SIZE:(p + 1) * PAGE_SIZE] for p in range(n)], axis=0)
        w = w + jnp.concatenate([bias] * n, axis=0)
        sp = _softplus2(w)
        if valid is not None:
            sp = jnp.where(valid, sp, 0.0)
        suf = jnp.dot(_split_bf16(sp), neg_u, preferred_element_type=F32)
        tot = jnp.sum(sp, axis=1, keepdims=True)
        carry = carry_ref[...]
        carries = []
        for p in range(n):
            carries.append(carry)
            carry = carry - tot[p * n_rows:(p + 1) * n_rows]
        carry_ref[...] = carry
        a = jnp.exp2(w - sp + suf + jnp.concatenate(carries, axis=0))
        if valid is not None:
            a = jnp.where(valid, a, 0.0)
        a = a.astype(BF16)
        a_cat = jnp.concatenate([a[p * n_rows:(p + 1) * n_rows] for p in range(n)], axis=1)
        v_cat = jnp.concatenate([v.astype(BF16) for v in v_pgs], axis=1)
        acc_ref[...] += lax.dot_general(a_cat, v_cat, (((1,), (1,)), ((), ())),
                                        preferred_element_type=F32)

    @pl.when(j == 0)
    def _():
        acc_ref[...] = jnp.zeros_like(acc_ref)
        carry_ref[...] = jnp.zeros_like(carry_ref)
        t = lax.broadcasted_iota(jnp.int32, (n_rows, PAGE_SIZE), 0) // H_B
        i = lax.broadcasted_iota(jnp.int32, (n_rows, PAGE_SIZE), 1)
        pages([knew_ref[0]], [vnew_ref[0]], i < t)

    pages([r[0] for r in k_refs], [r[0] for r in v_refs], None)

    @pl.when(j == pl.num_programs(1) - 1)
    def _():
        rr = lax.broadcasted_iota(jnp.int32, (n_rows, W_B), 0) % H_B
        cc = lax.broadcasted_iota(jnp.int32, (n_rows, W_B), 1) // DH_B
        own = jnp.where(rr == cc, acc_ref[...], 0.0)
        o_ref[0] = jnp.sum(own.reshape(n_q, H_B, W_B), axis=1)


def _sb_decode(page_table, qbd, bias_tile, k_new, v_new, pool_k, pool_v, *, pages_per_step):
    n_seq, n_pages = page_table.shape
    n_rows = qbd.shape[1]
    n_q = n_rows // H_B
    steps = n_pages // pages_per_step

    def page_spec(p):
        return pl.BlockSpec(
            (1, W_B, PAGE_SIZE),
            lambda b, j, pt, p=p: (pt[b, n_pages - 1 - (j * pages_per_step + p)], 0, 0))

    grid_spec = pltpu.PrefetchScalarGridSpec(
        num_scalar_prefetch=1, grid=(n_seq, steps),
        in_specs=[pl.BlockSpec((1, n_rows, W_B), lambda b, j, pt: (b, 0, 0)),
                  pl.BlockSpec((n_rows, LANES), lambda b, j, pt: (0, 0)),
                  pl.BlockSpec((1, W_B, PAGE_SIZE), lambda b, j, pt: (b, 0, 0)),
                  pl.BlockSpec((1, W_B, PAGE_SIZE), lambda b, j, pt: (b, 0, 0))]
                 + [page_spec(p) for p in range(pages_per_step)] * 2,
        out_specs=pl.BlockSpec((1, n_q, W_B), lambda b, j, pt: (b, 0, 0)),
        scratch_shapes=[pltpu.VMEM((n_rows, W_B), F32), pltpu.VMEM((n_rows, LANES), F32)])
    return pl.pallas_call(
        functools.partial(_sb_decode_kernel, pages_per_step=pages_per_step, n_q=n_q),
        grid_spec=grid_spec, out_shape=jax.ShapeDtypeStruct((n_seq, n_q, W_B), F32),
        compiler_params=_cparams(2), name="sb_decode",
    )(page_table, qbd, bias_tile, k_new, v_new, *([pool_k] * pages_per_step), *([pool_v] * pages_per_step))


def _out_ln_kernel(x_ref, a1_ref, a2_ref, w_ref, g_ref, b_ref, y_ref):
    a = jnp.concatenate([a1_ref[...], a2_ref[...]], axis=1)
    y = jnp.dot(a, w_ref[...], preferred_element_type=F32)
    y_ref[...] = _ln(ALPHA * x_ref[...] + y, g_ref[...], b_ref[...])


def _out_ln(x2d, a1, a2, w_bf, g, b, *, tm, a2_block_col=0, name):
    rows = x2d.shape[0]
    k1 = w_bf.shape[0] // 2
    return pl.pallas_call(
        _out_ln_kernel, grid=(rows // tm,),
        in_specs=[pl.BlockSpec((tm, D_MODEL), lambda i: (i, 0)),
                  pl.BlockSpec((tm, k1), lambda i: (i, 0)),
                  pl.BlockSpec((tm, k1), lambda i: (i, a2_block_col)),
                  _const_spec(w_bf.shape), _const_spec((1, D_MODEL)), _const_spec((1, D_MODEL))],
        out_specs=pl.BlockSpec((tm, D_MODEL), lambda i: (i, 0)),
        out_shape=jax.ShapeDtypeStruct((rows, D_MODEL), F32),
        compiler_params=_cparams(1), name=name,
    )(x2d, a1, a2, w_bf, g, b)


def _ffn_chunk(x_bf, wup_ref, cw_ref, cb_ref, ci, shift_fix):
    cols = slice(ci * FF_CHUNK, (ci + 1) * FF_CHUNK)
    vcols = slice(D_FF + ci * FF_CHUNK, D_FF + (ci + 1) * FF_CHUNK)
    g = jnp.dot(x_bf, wup_ref[:, cols], preferred_element_type=F32)
    val = jnp.dot(x_bf, wup_ref[:, vcols], preferred_element_type=F32)
    g1, g2 = shift_fix(g, cols)
    cw = cw_ref[:, cols]
    conv = cb_ref[:, cols] + cw[0:1] * g2 + cw[1:2] * g1 + cw[2:3] * g
    return g, (jax.nn.gelu(conv) * val).astype(BF16)


def _ffn_prompt_kernel(x_ref, wup_ref, cw_ref, cb_ref, wdn_ref, g_ref, b_ref,
                       y_ref, conv_ref, act_ref, prev_ref):
    tm = x_ref.shape[0]

    @pl.when(pl.program_id(1) == 0)
    def _():
        prev_ref[...] = jnp.zeros_like(prev_ref)

    x = x_ref[...]
    x_bf = x.astype(BF16)
    row = lax.broadcasted_iota(jnp.int32, (tm, FF_CHUNK), 0)

    def shift_fix(g, cols):
        p0 = prev_ref[0:1, cols]
        p1 = prev_ref[1:2, cols]
        g1 = jnp.where(row == 0, p1, pltpu.roll(g, 1, 0))
        g2 = jnp.where(row == 0, p0, jnp.where(row == 1, p1, pltpu.roll(g, 2, 0)))
        return g1, g2

    for ci in range(D_FF // FF_CHUNK):
        cols = slice(ci * FF_CHUNK, (ci + 1) * FF_CHUNK)
        g, act = _ffn_chunk(x_bf, wup_ref, cw_ref, cb_ref, ci, shift_fix)
        act_ref[:, cols] = act
        prev_ref[0:2, cols] = g[tm - 2:tm]
        conv_ref[0, :, cols] = g[tm - 2:tm]
    y = jnp.dot(act_ref[...], wdn_ref[...], preferred_element_type=F32)
    y_ref[...] = _ln(ALPHA * x + y, g_ref[...], b_ref[...])


def _ffn_sample_kernel(x_ref, wup_ref, cw_ref, cb_ref, wdn_ref, g_ref, b_ref, inj1_ref, inj2_ref,
                       y_ref, gate_ref, act_ref, *, seq_len):
    tm = x_ref.shape[0]
    x = x_ref[...]
    x_bf = x.astype(BF16)
    pos = lax.broadcasted_iota(jnp.int32, (tm, FF_CHUNK), 0) % seq_len

    def shift_fix(g, cols):
        g1 = jnp.where(pos >= 1, pltpu.roll(g, 1, 0), inj1_ref[:, cols])
        g2 = jnp.where(pos >= 2, pltpu.roll(g, 2, 0), inj2_ref[:, cols])
        return g1, g2

    for ci in range(D_FF // FF_CHUNK):
        cols = slice(ci * FF_CHUNK, (ci + 1) * FF_CHUNK)
        g, act = _ffn_chunk(x_bf, wup_ref, cw_ref, cb_ref, ci, shift_fix)
        act_ref[:, cols] = act
        gate_ref[:, cols] = g
    y = jnp.dot(act_ref[...], wdn_ref[...], preferred_element_type=F32)
    y_ref[...] = _ln(ALPHA * x + y, g_ref[...], b_ref[...])


def _ffn_weight_specs():
    return [_const_spec((D_MODEL, 2 * D_FF)), _const_spec((CONV_W, D_FF)), _const_spec((1, D_FF)),
            _const_spec((D_FF, D_MODEL)), _const_spec((1, D_MODEL)), _const_spec((1, D_MODEL))]


def _ffn_prompt(x2d, wup_bf, cw, cb, wdn_bf, g, b, *, tm, seq_len):
    rows = x2d.shape[0]
    batch = rows // seq_len
    tps = seq_len // tm
    return pl.pallas_call(
        _ffn_prompt_kernel, grid=(batch, tps),
        in_specs=[pl.BlockSpec((tm, D_MODEL), lambda bi, t: (bi * tps + t, 0))] + _ffn_weight_specs(),
        out_specs=[pl.BlockSpec((tm, D_MODEL), lambda bi, t: (bi * tps + t, 0)),
                   pl.BlockSpec((1, CONV_W - 1, D_FF), lambda bi, t: (bi, 0, 0))],
        out_shape=[jax.ShapeDtypeStruct((rows, D_MODEL), F32),
                   jax.ShapeDtypeStruct((batch, CONV_W - 1, D_FF), F32)],
        scratch_shapes=[pltpu.VMEM((tm, D_FF), BF16), pltpu.VMEM((8, D_FF), F32)],
        compiler_params=_cparams(2), name="ffn_prompt",
    )(x2d, wup_bf, cw, cb, wdn_bf, g, b)


def _ffn_sample(x2d, wup_bf, cw, cb, wdn_bf, g, b, conv_state, *, seq_len):
    rows = x2d.shape[0]
    n_seq = rows // seq_len
    zeros = jnp.zeros((n_seq, 1, D_FF), F32)
    inj1 = jnp.concatenate([conv_state[:, 1:2]] + [zeros] * (seq_len - 1), axis=1).reshape(rows, D_FF)
    inj2 = jnp.concatenate([conv_state] + [zeros] * (seq_len - 2), axis=1).reshape(rows, D_FF)
    full = lambda width: pl.BlockSpec((rows, width), lambda i: (0, 0))
    y, gate = pl.pallas_call(
        functools.partial(_ffn_sample_kernel, seq_len=seq_len), grid=(1,),
        in_specs=[full(D_MODEL)] + _ffn_weight_specs() + [full(D_FF), full(D_FF)],
        out_specs=[full(D_MODEL), full(D_FF)],
        out_shape=[jax.ShapeDtypeStruct((rows, D_MODEL), F32), jax.ShapeDtypeStruct((rows, D_FF), F32)],
        scratch_shapes=[pltpu.VMEM((rows, D_FF), BF16)],
        compiler_params=_cparams(1), name="ffn_sample",
    )(x2d, wup_bf, cw, cb, wdn_bf, g, b, inj1, inj2)
    new_state = gate.reshape(n_seq, seq_len, D_FF)[:, seq_len - (CONV_W - 1):]
    return y, new_state


def _ret_kernel(x_ref, w_ref, cos_ref, sin_ref, gg_ref, gb_ref, *refs, decay_len, has_state):
    if has_state:
        s0_ref, y_ref, sout_ref, s_ref = refs
    else:
        y_ref, sout_ref, s_ref = refs
    tm = x_ref.shape[0]
    t = pl.program_id(1)

    @pl.when(t == 0)
    def _():
        if has_state:
            s_ref[...] = s0_ref[0]
        else:
            s_ref[...] = jnp.zeros_like(s_ref)

    x = x_ref[...].astype(BF16)
    cos = cos_ref[...]
    sin = sin_ref[...]
    half = DK_C // 2

    def rotary(a):
        a1, a2 = a[:, :half], a[:, half:]
        return jnp.concatenate([a1 * cos - a2 * sin, a1 * sin + a2 * cos], axis=1)

    ii = lax.broadcasted_iota(jnp.int32, (RET_CHUNK, RET_CHUNK), 0)
    jj = lax.broadcasted_iota(jnp.int32, (RET_CHUNK, RET_CHUNK), 1)
    diff = jnp.where(ii >= jj, ii - jj, 0).astype(F32)
    idx = lax.broadcasted_iota(jnp.int32, (RET_CHUNK, 1), 0).astype(F32)
    v_off = 2 * H_C * DK_C
    g_off = v_off + H_C * DV_C
    for h in range(H_C):
        log_g = _LOG_G[h]
        dmat = jnp.where(ii >= jj, jnp.exp(diff * log_g), 0.0)
        q_dec = jnp.exp((idx + 1.0) * log_g)
        k_dec = jnp.exp((decay_len - 1.0 - idx) * log_g)
        c_dec = math.exp(decay_len * log_g)
        q = rotary(jnp.dot(x, w_ref[:, h * DK_C:(h + 1) * DK_C], preferred_element_type=F32))
        k = rotary(jnp.dot(x, w_ref[:, (H_C + h) * DK_C:(H_C + h + 1) * DK_C],
                           preferred_element_type=F32)) * (DK_C ** -0.5)
        v = jnp.dot(x, w_ref[:, v_off + h * DV_C:v_off + (h + 1) * DV_C], preferred_element_type=F32)
        gate = jnp.dot(x, w_ref[:, g_off + h * DV_C:g_off + (h + 1) * DV_C], preferred_element_type=F32)
        vcols = slice(h * DV_C, (h + 1) * DV_C)
        for ci in range(tm // RET_CHUNK):
            rows = slice(ci * RET_CHUNK, (ci + 1) * RET_CHUNK)
            qc = q[rows].astype(BF16)
            kc = k[rows]
            vc = v[rows].astype(BF16)
            state = s_ref[h]
            sc = lax.dot_general(qc, kc.astype(BF16), (((1,), (1,)), ((), ())),
                                 preferred_element_type=F32) * dmat
            o = (jnp.dot(sc.astype(BF16), vc, preferred_element_type=F32)
                 + jnp.dot(qc, state.astype(BF16), preferred_element_type=F32) * q_dec)
            kd = (kc * k_dec).astype(BF16)
            s_ref[h] = state * c_dec + lax.dot_general(
                kd, vc, (((0,), (0,)), ((), ())), preferred_element_type=F32)
            mu = jnp.mean(o, axis=-1, keepdims=True)
            oc = o - mu
            var = jnp.mean(oc * oc, axis=-1, keepdims=True)
            normed = oc * lax.rsqrt(var + LN_EPS) * gg_ref[:, vcols] + gb_ref[:, vcols]
            y_ref[rows, vcols] = (jax.nn.silu(gate[rows]) * normed).astype(y_ref.dtype)

    @pl.when(t == pl.num_programs(1) - 1)
    def _():
        sout_ref[0] = s_ref[...]


def _retention(x2d, w_bf, cos, sin, gg, gb, state0, *, tm, seq_len, decay_len, name):
    rows = x2d.shape[0]
    n_seq = rows // seq_len
    tps = seq_len // tm
    d_y = H_C * DV_C
    has_state = state0 is not None
    state_spec = pl.BlockSpec((1, H_C, DK_C, DV_C), lambda bi, t: (bi, 0, 0, 0))
    in_specs = [pl.BlockSpec((tm, D_MODEL), lambda bi, t: (bi * tps + t, 0)),
                _const_spec(w_bf.shape),
                pl.BlockSpec((tm, DK_C // 2), lambda bi, t: (t, 0)),
                pl.BlockSpec((tm, DK_C // 2), lambda bi, t: (t, 0)),
                _const_spec((1, d_y)), _const_spec((1, d_y))]
    args = [x2d, w_bf, cos, sin, gg, gb]
    if has_state:
        in_specs.append(state_spec)
        args.append(state0)
    return pl.pallas_call(
        functools.partial(_ret_kernel, decay_len=decay_len, has_state=has_state),
        grid=(n_seq, tps), in_specs=in_specs,
        out_specs=[pl.BlockSpec((tm, d_y), lambda bi, t: (bi * tps + t, 0)), state_spec],
        out_shape=[jax.ShapeDtypeStruct((rows, d_y), BF16),
                   jax.ShapeDtypeStruct((n_seq, H_C, DK_C, DV_C), F32)],
        scratch_shapes=[pltpu.VMEM((H_C, DK_C, DV_C), F32)],
        compiler_params=_cparams(2), name=name,
    )(*args)


def _rope_tables(pos):
    half = DK_C // 2
    inv = ROPE_BASE ** (-jnp.arange(half, dtype=F32) / half)
    ang = pos.astype(F32)[:, None] * inv[None, :]
    return jnp.cos(ang), jnp.sin(ang)


def kernel(x_prompt, x_sample, cache_sb_k, cache_sb_v, state_ret, state_ffn_conv, page_table,
           w_in_ab, vln_g, vln_b, w_s, b_s, sb_bias, w_out_ab, w_in_ret, gn_g, gn_b, w_out_ret,
           ln1_g, ln1_b, ln2_g, ln2_b, w_up, conv_w, conv_b, w_down):
    bp, tp, _ = x_prompt.shape
    bs, ts, _ = x_sample.shape
    n_pages = page_table.shape[1]
    past_len = n_pages * PAGE_SIZE
    n_phys = cache_sb_k.shape[1]
    xp = x_prompt.reshape(bp * tp, D_MODEL)
    xs = x_sample.reshape(bs * ts, D_MODEL)
    row2d = lambda a: a.reshape(1, -1)

    sb_k_p, sb_v_p, sb_k_s, sb_v_s, chunk_v_s = [], [], [], [], []
    ret_p, ret_s, conv_p, conv_s = [], [], [], []
    for l in range(DEPTH):
        if l % 2 == 0:
            e = l // 2
            w_in = w_in_ab[e].astype(BF16)
            vg, vb = row2d(vln_g[e]), row2d(vln_b[e])
            mixb_p = jnp.repeat(b_s[e].T, DG_A, axis=1)
            a_p, q_hm, k_hm, v_hm, kf_p, vf_p = _ab_proj(
                xp, w_in, vg, vb, w_s[e], mixb_p, mix_len=CHUNK, head_major=True, tm=512, seq_len=tp)
            bias2 = sb_bias[e] * LOG2E
            b_p = _sb_attn_prompt(q_hm, k_hm, v_hm, bias2, blk=256)

            reps = CHUNK // ts
            wmix_s = jnp.tile(w_s[e][:, :ts, :ts], (1, reps, reps))
            mixb_s = jnp.repeat(jnp.tile(b_s[e][:, :ts], (1, reps)).T, DG_A, axis=1)
            a_s, q_s, kf_s, vf_s, va_s = _ab_proj(
                xs, w_in, vg, vb, wmix_s, mixb_s, mix_len=ts, head_major=False, tm=bs * ts)
            head_of_col = jnp.arange(W_B) // DH_B
            own = (jnp.arange(H_B)[:, None] == head_of_col[None, :]).astype(F32)
            qbd = (q_s.reshape(bs, ts, 1, W_B) * own[None, None]).reshape(bs, ts * H_B, W_B).astype(BF16)
            bias_tile = jnp.broadcast_to(jnp.tile(bias2, ts)[:, None], (ts * H_B, LANES))
            pad = ((0, 0), (0, 0), (0, PAGE_SIZE - ts))
            k_new = jnp.pad(kf_s.reshape(bs, ts, W_B).transpose(0, 2, 1), pad)
            v_new = jnp.pad(vf_s.reshape(bs, ts, W_B).transpose(0, 2, 1), pad)
            pool_k = cache_sb_k[e].transpose(0, 2, 3, 1).reshape(n_phys, W_B, PAGE_SIZE)
            pool_v = cache_sb_v[e].transpose(0, 2, 3, 1).reshape(n_phys, W_B, PAGE_SIZE)
            b_s_out = _sb_decode(page_table, qbd, bias_tile, k_new, v_new, pool_k, pool_v,
                                 pages_per_step=16)

            w_out = w_out_ab[e].astype(BF16)
            g1, b1 = row2d(ln1_g[l]), row2d(ln1_b[l])
            xp = _out_ln(xp, a_p, b_p.reshape(bp * tp, W_B), w_out, g1, b1, tm=512, name="out_ln_ab_prompt")
            xs = _out_ln(xs, a_s, b_s_out.reshape(bs * ts, W_B).astype(BF16), w_out, g1, b1,
                         tm=bs * ts, name="out_ln_ab_sample")
            pages_of = lambda a: a.reshape(bp, tp // PAGE_SIZE, H_B, DH_B, PAGE_SIZE).transpose(0, 1, 4, 2, 3)
            sb_k_p.append(pages_of(kf_p))
            sb_v_p.append(pages_of(vf_p))
            sb_k_s.append(kf_s.reshape(bs, ts, H_B, DH_B))
            sb_v_s.append(vf_s.reshape(bs, ts, H_B, DH_B))
            chunk_v_s.append(va_s.reshape(bs, ts, G_A, DG_A))
        else:
            o = l // 2
            w_in = w_in_ret[o].astype(BF16)
            gg, gb = row2d(gn_g[o]), row2d(gn_b[o])
            cos_p, sin_p = _rope_tables(jnp.arange(tp))
            y_p, s_p = _retention(xp, w_in, cos_p, sin_p, gg, gb, None, tm=256, seq_len=tp,
                                  decay_len=RET_CHUNK, name="retention_prompt")
            cos_s, sin_s = _rope_tables(past_len + jnp.arange(RET_CHUNK))
            xs_pad = jnp.pad(xs.reshape(bs, ts, D_MODEL), ((0, 0), (0, RET_CHUNK - ts), (0, 0)))
            y_s, s_s = _retention(xs_pad.reshape(bs * RET_CHUNK, D_MODEL), w_in, cos_s, sin_s, gg, gb,
                                  state_ret[o], tm=RET_CHUNK, seq_len=RET_CHUNK, decay_len=ts,
                                  name="retention_sample")
            y_s = y_s.reshape(bs, RET_CHUNK, H_C * DV_C)[:, :ts].reshape(bs * ts, H_C * DV_C)
            w_out = w_out_ret[o].astype(BF16)
            g1, b1 = row2d(ln1_g[l]), row2d(ln1_b[l])
            xp = _out_ln(xp, y_p, y_p, w_out, g1, b1, tm=512, a2_block_col=1, name="out_ln_ret_prompt")
            xs = _out_ln(xs, y_s, y_s, w_out, g1, b1, tm=bs * ts, a2_block_col=1, name="out_ln_ret_sample")
            ret_p.append(s_p)
            ret_s.append(s_s)

        wup = w_up[l].astype(BF16)
        wdn = w_down[l].astype(BF16)
        g2, b2 = row2d(ln2_g[l]), row2d(ln2_b[l])
        xp, cp = _ffn_prompt(xp, wup, conv_w[l], row2d(conv_b[l]), wdn, g2, b2, tm=512, seq_len=tp)
        xs, cs = _ffn_sample(xs, wup, conv_w[l], row2d(conv_b[l]), wdn, g2, b2, state_ffn_conv[l], seq_len=ts)
        conv_p.append(cp)
        conv_s.append(cs)

    return (xp.reshape(bp, tp, D_MODEL), xs.reshape(bs, ts, D_MODEL),
            jnp.stack(sb_k_p), jnp.stack(sb_v_p), jnp.stack(sb_k_s), jnp.stack(sb_v_s),
            jnp.stack(chunk_v_s), jnp.stack(ret_p), jnp.stack(ret_s), jnp.stack(conv_p), jnp.stack(conv_s))
```

```python
import functools
import math

import jax
import jax.numpy as jnp
from jax import lax
from jax.experimental import pallas as pl
from jax.experimental.pallas import tpu as pltpu

F32 = jnp.float32
BF16 = jnp.bfloat16

D_MODEL = 1024
W_A = D_MODEL // 2
G_A = 8
DG_A = W_A // G_A
CHUNK = 128
W_B = D_MODEL // 2
H_B = 8
DH_B = W_B // H_B
PAGE_SIZE = 128
H_C = 4
DK_C = D_MODEL // H_C
DV_C = 2 * DK_C
RET_CHUNK = 128
ROPE_BASE = 10000.0
D_FF = ((8 * D_MODEL // 3 + 127) // 128) * 128
CONV_W = 3
DEPTH = 2
ALPHA = (2 * DEPTH) ** 0.25
LN_EPS = 1e-5
LOG2E = 1.4426950408889634
EXP2_CLAMP = 126.0
BIAS_PIECES = 3

LANES = 128
FF_CHUNK = 256
VMEM_LIMIT = 56 * 1024 * 1024

_LOG_G = tuple(math.log1p(-(2.0 ** (-5.0 - h))) for h in range(H_C))


def _cparams(n_axes):
    return pltpu.CompilerParams(
        dimension_semantics=("arbitrary",) * n_axes, vmem_limit_bytes=VMEM_LIMIT)


def _const_spec(shape):
    nd = len(shape)
    return pl.BlockSpec(shape, lambda *_: (0,) * nd, pipeline_mode=pl.Buffered(1))


def _ln(x, g, b):
    mu = jnp.mean(x, axis=-1, keepdims=True)
    xc = x - mu
    var = jnp.mean(xc * xc, axis=-1, keepdims=True)
    return xc * lax.rsqrt(var + LN_EPS) * g + b


def _softplus2(w):
    return jnp.maximum(w, jnp.log2(1.0 + jnp.exp2(jnp.minimum(w, EXP2_CLAMP))))


def _split_bf16(x):
    hi = x.astype(BF16)
    lo = (x - hi.astype(F32)).astype(BF16)
    return jnp.concatenate([hi, lo], axis=1)


def _emit_skewed(chains):
    depth = max(len(chain) for chain in chains)
    for t in range(len(chains) + depth - 1):
        for s in range(depth):
            if 0 <= t - s < len(chains) and s < len(chains[t - s]):
                chains[t - s][s]()


def _neg_suffix_matrix(n, parts, inclusive):
    j = lax.broadcasted_iota(jnp.int32, (parts * n, n), 0) % n
    s = lax.broadcasted_iota(jnp.int32, (parts * n, n), 1)
    return jnp.where((j >= s) if inclusive else (j > s), -1.0, 0.0).astype(BF16)


def _ab_proj_kernel(x_ref, w_ref, vg_ref, vb_ref, wmix_ref, mixb_ref, *refs, mix_len, head_major):
    if head_major:
        a_ref, q_ref, k_ref, v_ref, kf_ref, vf_ref, mixed_ref = refs
    else:
        a_ref, q_ref, kf_ref, vf_ref, va_ref, mixed_ref = refs
    tm = x_ref.shape[0]
    x = x_ref[...].astype(BF16)
    h = jnp.dot(x, w_ref[...], preferred_element_type=F32)
    u = jax.nn.gelu(h[:, :W_A])
    va = _ln(jax.nn.gelu(h[:, W_A:2 * W_A]), vg_ref[...], vb_ref[...])
    q = h[:, 2 * W_A:2 * W_A + W_B] * (DH_B ** -0.5 * LOG2E)
    k = h[:, 2 * W_A + W_B:2 * W_A + 2 * W_B]
    v = h[:, 2 * W_A + 2 * W_B:]
    if head_major:
        for p in range(W_B // LANES):
            sl = slice(p * LANES, (p + 1) * LANES)
            q_ref[0, p] = q[:, sl].astype(BF16)
            k_ref[0, p] = k[:, sl].astype(BF16)
            v_ref[0, p] = v[:, sl].astype(BF16)
        for pg in range(tm // PAGE_SIZE):
            rows = slice(pg * PAGE_SIZE, (pg + 1) * PAGE_SIZE)
            kf_ref[pg] = k[rows].T
            vf_ref[pg] = v[rows].T
    else:
        kf_ref[...] = k
        vf_ref[...] = v
        q_ref[...] = q
        va_ref[...] = va

    r = lax.broadcasted_iota(jnp.int32, (CHUNK, CHUNK), 0)
    c = lax.broadcasted_iota(jnp.int32, (CHUNK, CHUNK), 1)
    if mix_len == CHUNK:
        mask = c <= r
    else:
        mask = jnp.logical_and(r // mix_len == c // mix_len, c % mix_len <= r % mix_len)
    wg = [jnp.where(mask, wmix_ref[g], 0.0).astype(BF16) for g in range(G_A)]
    for ci in range(tm // CHUNK):
        rows = slice(ci * CHUNK, (ci + 1) * CHUNK)
        va_c = va[rows]
        for g in range(G_A):
            cols = slice(g * DG_A, (g + 1) * DG_A)
            mixed_ref[:, cols] = jnp.dot(wg[g], va_c[:, cols].astype(BF16), preferred_element_type=F32)
        a_ref[rows, :] = (u[rows] * (mixed_ref[...] + mixb_ref[...])).astype(BF16)


def _ab_proj(x2d, w_bf, vg, vb, wmix, mixb, *, mix_len, head_major, tm, seq_len=None):
    rows = x2d.shape[0]
    d_in = w_bf.shape[1]
    n_tiles = rows // tm
    row_spec = lambda width: pl.BlockSpec((tm, width), lambda i: (i, 0))
    in_specs = [row_spec(D_MODEL), _const_spec((D_MODEL, d_in)), _const_spec((1, W_A)),
                _const_spec((1, W_A)), _const_spec((G_A, CHUNK, CHUNK)), _const_spec((CHUNK, W_A))]
    if head_major:
        tps = seq_len // tm
        batch = rows // seq_len
        hm_spec = pl.BlockSpec((1, W_B // LANES, tm, LANES), lambda i: (i // tps, 0, i % tps, 0))
        hm_shape = jax.ShapeDtypeStruct((batch, W_B // LANES, seq_len, LANES), BF16)
        ppt = tm // PAGE_SIZE
        pg_spec = pl.BlockSpec((ppt, W_B, PAGE_SIZE), lambda i: (i, 0, 0))
        pg_shape = jax.ShapeDtypeStruct((rows // PAGE_SIZE, W_B, PAGE_SIZE), F32)
        out_specs = [row_spec(W_A), hm_spec, hm_spec, hm_spec, pg_spec, pg_spec]
        out_shape = [jax.ShapeDtypeStruct((rows, W_A), BF16), hm_shape, hm_shape, hm_shape,
                     pg_shape, pg_shape]
    else:
        out_specs = [row_spec(W_A), row_spec(W_B), row_spec(W_B), row_spec(W_B), row_spec(W_A)]
        out_shape = [jax.ShapeDtypeStruct((rows, W_A), BF16)] + [
            jax.ShapeDtypeStruct((rows, W_B), F32)] * 3 + [jax.ShapeDtypeStruct((rows, W_A), F32)]
    return pl.pallas_call(
        functools.partial(_ab_proj_kernel, mix_len=mix_len, head_major=head_major),
        grid=(n_tiles,), in_specs=in_specs, out_specs=out_specs, out_shape=out_shape,
        scratch_shapes=[pltpu.VMEM((CHUNK, W_A), F32)],
        compiler_params=_cparams(1),
        name="ab_proj_prompt" if head_major else "ab_proj_sample",
    )(x2d, w_bf, vg, vb, wmix, mixb)


def _sb_attn_kernel(bias_ref, q_ref, k_ref, v_ref, o_ref, acc_ref, carry_ref, *, blk):
    qi = pl.program_id(1)
    n_pairs = q_ref.shape[1]
    n_heads = 2 * n_pairs
    lane = lax.broadcasted_iota(jnp.int32, (blk, LANES), 1)
    neg_u = _neg_suffix_matrix(blk, 1, inclusive=False)
    row = lax.broadcasted_iota(jnp.int32, (blk, blk), 0)
    col = lax.broadcasted_iota(jnp.int32, (blk, blk), 1)
    causal = col < row
    k_ones = jnp.where(lane < BIAS_PIECES, 1.0, 0.0).astype(BF16)
    q_aug = []
    for h in range(n_heads):
        q2 = q_ref[0, h // 2]
        qh = jnp.where((lane >= DH_B) == bool(h % 2), q2, jnp.zeros_like(q2))
        q_aug.append(jnp.concatenate(
            [qh, jnp.broadcast_to(bias_ref[h:h + 1, :], (blk, LANES))], axis=1))

    def start_of(kb):
        return kb * blk if isinstance(kb, int) else pl.multiple_of(kb * blk, blk)

    def block_chains(kb, diagonal):
        start = start_of(kb)
        vals = [dict() for _ in range(n_heads)]

        def logits(h):
            k2 = k_ref[0, h // 2, pl.ds(start, blk), :]
            vals[h]["w"] = lax.dot_general(q_aug[h], jnp.concatenate([k2, k_ones], axis=1),
                                           (((1,), (1,)), ((), ())), preferred_element_type=F32)

        def keep(h):
            w = vals[h].pop("w")
            sp = _softplus2(w)
            if diagonal:
                sp = jnp.where(causal, sp, 0.0)
                vals[h]["d"] = w - sp
                carry_ref[h] = jnp.broadcast_to(-jnp.sum(sp, axis=1, keepdims=True), (blk, LANES))
            else:
                carry = carry_ref[h]
                vals[h]["d"] = w - sp + jnp.concatenate([carry] * (blk // LANES), axis=1)
                carry_ref[h] = carry - jnp.sum(sp, axis=1, keepdims=True)
            vals[h]["sp"] = sp.astype(BF16)

        def suffix(h):
            vals[h]["suf"] = jnp.dot(vals[h].pop("sp"), neg_u, preferred_element_type=F32)

        def weights(h):
            a = jnp.exp2(vals[h].pop("d") + vals[h].pop("suf"))
            if diagonal:
                a = jnp.where(causal, a, 0.0)
            vals[h]["a"] = a.astype(BF16)

        def values(h):
            v2 = v_ref[0, h // 2, pl.ds(start, blk), :]
            pv = jnp.dot(vals[h].pop("a"), v2, preferred_element_type=F32)
            if diagonal:
                acc_ref[h] = pv
            else:
                acc_ref[h] += pv

        return [[functools.partial(stage, h) for stage in (logits, keep, suffix, weights, values)]
                for h in range(n_heads)]

    def blocks(kbs, diagonal):
        _emit_skewed([chain for kb in kbs for chain in block_chains(kb, diagonal)])

    blocks([qi], True)

    def pair(j, _):
        blocks([qi - 1 - 2 * j, qi - 2 - 2 * j], False)
        return 0

    lax.fori_loop(0, qi // 2, pair, 0)

    @pl.when(qi % 2 == 1)
    def _():
        blocks([0], False)

    for hp in range(n_pairs):
        o_ref[0, :, hp * LANES:(hp + 1) * LANES] = jnp.where(
            lane < DH_B, acc_ref[2 * hp], acc_ref[2 * hp + 1]).astype(o_ref.dtype)


def _bias_rows(bias2):
    pieces, rest = [], bias2.astype(F32)
    for _ in range(BIAS_PIECES):
        piece = rest.astype(BF16)
        pieces.append(piece)
        rest = rest - piece.astype(F32)
    rows = jnp.stack(pieces, axis=1)
    return jnp.pad(rows, ((0, 0), (0, LANES - BIAS_PIECES)))


def _sb_attn_prompt(q_hm, k_hm, v_hm, bias2, *, blk):
    batch, n_pairs, seq_len, _ = q_hm.shape
    nq = seq_len // blk
    n_heads = 2 * n_pairs
    return pl.pallas_call(
        functools.partial(_sb_attn_kernel, blk=blk), grid=(batch, nq),
        in_specs=[pl.BlockSpec((n_heads, LANES), lambda b, i: (0, 0)),
                  pl.BlockSpec((1, n_pairs, blk, LANES), lambda b, i: (b, 0, i, 0)),
                  pl.BlockSpec((1, n_pairs, seq_len, LANES), lambda b, i: (b, 0, 0, 0)),
                  pl.BlockSpec((1, n_pairs, seq_len, LANES), lambda b, i: (b, 0, 0, 0))],
        out_specs=pl.BlockSpec((1, blk, n_pairs * LANES), lambda b, i: (b, i, 0)),
        out_shape=jax.ShapeDtypeStruct((batch, seq_len, n_pairs * LANES), BF16),
        scratch_shapes=[pltpu.VMEM((n_heads, blk, LANES), F32), pltpu.VMEM((n_heads, blk, LANES), F32)],
        compiler_params=_cparams(2), name="sb_attn_prompt",
    )(_bias_rows(bias2), q_hm, k_hm, v_hm)


def _sb_decode_kernel(pt_ref, qbd_ref, bias_ref, knew_ref, vnew_ref, *refs, pages_per_step, n_q):
    del pt_ref
    k_refs = refs[:pages_per_step]
    v_refs = refs[pages_per_step:2 * pages_per_step]
    o_ref, acc_ref, carry_ref = refs[2 * pages_per_step:]
    j = pl.program_id(1)
    n_rows = qbd_ref.shape[1]
    qbd = qbd_ref[0]
    bias = bias_ref[...]
    neg_u = _neg_suffix_matrix(PAGE_SIZE, 2, inclusive=True)

    def pages(k_pgs, v_pgs, valid):
        n = len(k_pgs)
        k_cat = jnp.concatenate([k.astype(BF16) for k in k_pgs], axis=1)
        w_cat = jnp.dot(qbd, k_cat, preferred_element_type=F32)
        w = jnp.concatenate([w_cat[:, p * PAGE_SIZE:(p + 1) * PAGE_SIZE] for p in range(n)], axis=0)
        w = w + jnp.concatenate([bias] * n, axis=0)
        sp = _softplus2(w)
        if valid is not None:
            sp = jnp.where(valid, sp, 0.0)
        suf = jnp.dot(_split_bf16(sp), neg_u, preferred_element_type=F32)
        tot = jnp.sum(sp, axis=1, keepdims=True)
        carry = carry_ref[...]
        carries = []
        for p in range(n):
            carries.append(carry)
            carry = carry - tot[p * n_rows:(p + 1) * n_rows]
        carry_ref[...] = carry
        a = jnp.exp2(w + suf + jnp.concatenate(carries, axis=0))
        if valid is not None:
            a = jnp.where(valid, a, 0.0)
        a = a.astype(BF16)
        a_cat = jnp.concatenate([a[p * n_rows:(p + 1) * n_rows] for p in range(n)], axis=1)
        v_cat = jnp.concatenate([v.astype(BF16) for v in v_pgs], axis=1)
        acc_ref[...] += lax.dot_general(a_cat, v_cat, (((1,), (1,)), ((), ())),
                                        preferred_element_type=F32)

    @pl.when(j == 0)
    def _():
        acc_ref[...] = jnp.zeros_like(acc_ref)
        carry_ref[...] = jnp.zeros_like(carry_ref)
        t = lax.broadcasted_iota(jnp.int32, (n_rows, PAGE_SIZE), 0) // H_B
        i = lax.broadcasted_iota(jnp.int32, (n_rows, PAGE_SIZE), 1)
        pages([knew_ref[0]], [vnew_ref[0]], i < t)

    pages([r[0] for r in k_refs], [r[0] for r in v_refs], None)

    @pl.when(j == pl.num_programs(1) - 1)
    def _():
        rr = lax.broadcasted_iota(jnp.int32, (n_rows, W_B), 0) % H_B
        cc = lax.broadcasted_iota(jnp.int32, (n_rows, W_B), 1) // DH_B
        own = jnp.where(rr == cc, acc_ref[...], 0.0)
        o_ref[0] = jnp.sum(own.reshape(n_q, H_B, W_B), axis=1)


def _sb_decode(page_table, qbd, bias_tile, k_new, v_new, pool_k, pool_v, *, pages_per_step):
    n_seq, n_pages = page_table.shape
    n_rows = qbd.shape[1]
    n_q = n_rows // H_B
    steps = n_pages // pages_per_step

    def page_spec(p):
        return pl.BlockSpec(
            (1, W_B, PAGE_SIZE),
            lambda b, j, pt, p=p: (pt[b, n_pages - 1 - (j * pages_per_step + p)], 0, 0))

    grid_spec = pltpu.PrefetchScalarGridSpec(
        num_scalar_prefetch=1, grid=(n_seq, steps),
        in_specs=[pl.BlockSpec((1, n_rows, W_B), lambda b, j, pt: (b, 0, 0)),
                  pl.BlockSpec((n_rows, LANES), lambda b, j, pt: (0, 0)),
                  pl.BlockSpec((1, W_B, PAGE_SIZE), lambda b, j, pt: (b, 0, 0)),
                  pl.BlockSpec((1, W_B, PAGE_SIZE), lambda b, j, pt: (b, 0, 0))]
                 + [page_spec(p) for p in range(pages_per_step)] * 2,
        out_specs=pl.BlockSpec((1, n_q, W_B), lambda b, j, pt: (b, 0, 0)),
        scratch_shapes=[pltpu.VMEM((n_rows, W_B), F32), pltpu.VMEM((n_rows, LANES), F32)])
    return pl.pallas_call(
        functools.partial(_sb_decode_kernel, pages_per_step=pages_per_step, n_q=n_q),
        grid_spec=grid_spec, out_shape=jax.ShapeDtypeStruct((n_seq, n_q, W_B), F32),
        compiler_params=_cparams(2), name="sb_decode",
    )(page_table, qbd, bias_tile, k_new, v_new, *([pool_k] * pages_per_step), *([pool_v] * pages_per_step))


def _mixer_out_ln(x_ref, a1_ref, a2_ref, wo_ref, g_ref, b_ref):
    a = jnp.concatenate([a1_ref[...], a2_ref[...]], axis=1)
    y = jnp.dot(a, wo_ref[...], preferred_element_type=F32)
    return _ln(ALPHA * x_ref[...] + y, g_ref[...], b_ref[...])


def _mixer_specs(tm, row_map, k1, a2_block_col, w_shape):
    i0 = lambda *idx: (row_map(*idx), 0)
    i2 = lambda *idx: (row_map(*idx), a2_block_col)
    return [pl.BlockSpec((tm, D_MODEL), i0), pl.BlockSpec((tm, k1), i0), pl.BlockSpec((tm, k1), i2),
            _const_spec(w_shape), _const_spec((1, D_MODEL)), _const_spec((1, D_MODEL))]


def _ffn_chunk(x_bf, wup_ref, cw_ref, cb_ref, ci, shift_fix):
    cols = slice(ci * FF_CHUNK, (ci + 1) * FF_CHUNK)
    vcols = slice(D_FF + ci * FF_CHUNK, D_FF + (ci + 1) * FF_CHUNK)
    g = jnp.dot(x_bf, wup_ref[:, cols], preferred_element_type=F32)
    val = jnp.dot(x_bf, wup_ref[:, vcols], preferred_element_type=F32)
    g1, g2 = shift_fix(g, cols)
    cw = cw_ref[:, cols]
    conv = cb_ref[:, cols] + cw[0:1] * g2 + cw[1:2] * g1 + cw[2:3] * g
    return g, (jax.nn.gelu(conv) * val).astype(BF16)


def _ffn_prompt_kernel(x_ref, a1_ref, a2_ref, wo_ref, g1_ref, b1_ref,
                       wup_ref, cw_ref, cb_ref, wdn_ref, g_ref, b_ref,
                       y_ref, conv_ref, act_ref, prev_ref):
    tm = x_ref.shape[0]

    @pl.when(pl.program_id(1) == 0)
    def _():
        prev_ref[...] = jnp.zeros_like(prev_ref)

    x = _mixer_out_ln(x_ref, a1_ref, a2_ref, wo_ref, g1_ref, b1_ref)
    x_bf = x.astype(BF16)
    row = lax.broadcasted_iota(jnp.int32, (tm, FF_CHUNK), 0)

    def shift_fix(g, cols):
        p0 = prev_ref[0:1, cols]
        p1 = prev_ref[1:2, cols]
        g1 = jnp.where(row == 0, p1, pltpu.roll(g, 1, 0))
        g2 = jnp.where(row == 0, p0, jnp.where(row == 1, p1, pltpu.roll(g, 2, 0)))
        return g1, g2

    for ci in range(D_FF // FF_CHUNK):
        cols = slice(ci * FF_CHUNK, (ci + 1) * FF_CHUNK)
        g, act = _ffn_chunk(x_bf, wup_ref, cw_ref, cb_ref, ci, shift_fix)
        act_ref[:, cols] = act
        prev_ref[0:2, cols] = g[tm - 2:tm]
        conv_ref[0, :, cols] = g[tm - 2:tm]
    y = jnp.dot(act_ref[...], wdn_ref[...], preferred_element_type=F32)
    y_ref[...] = _ln(ALPHA * x + y, g_ref[...], b_ref[...])


def _ffn_sample_kernel(x_ref, a1_ref, a2_ref, wo_ref, g1_ref, b1_ref,
                       wup_ref, cw_ref, cb_ref, wdn_ref, g_ref, b_ref, inj1_ref, inj2_ref,
                       y_ref, gate_ref, act_ref, *, seq_len):
    tm = x_ref.shape[0]
    x = _mixer_out_ln(x_ref, a1_ref, a2_ref, wo_ref, g1_ref, b1_ref)
    x_bf = x.astype(BF16)
    pos = lax.broadcasted_iota(jnp.int32, (tm, FF_CHUNK), 0) % seq_len

    def shift_fix(g, cols):
        g1 = jnp.where(pos >= 1, pltpu.roll(g, 1, 0), inj1_ref[:, cols])
        g2 = jnp.where(pos >= 2, pltpu.roll(g, 2, 0), inj2_ref[:, cols])
        return g1, g2

    for ci in range(D_FF // FF_CHUNK):
        cols = slice(ci * FF_CHUNK, (ci + 1) * FF_CHUNK)
        g, act = _ffn_chunk(x_bf, wup_ref, cw_ref, cb_ref, ci, shift_fix)
        act_ref[:, cols] = act
        gate_ref[:, cols] = g
    y = jnp.dot(act_ref[...], wdn_ref[...], preferred_element_type=F32)
    y_ref[...] = _ln(ALPHA * x + y, g_ref[...], b_ref[...])


def _ffn_weight_specs():
    return [_const_spec((D_MODEL, 2 * D_FF)), _const_spec((CONV_W, D_FF)), _const_spec((1, D_FF)),
            _const_spec((D_FF, D_MODEL)), _const_spec((1, D_MODEL)), _const_spec((1, D_MODEL))]


def _ffn_prompt(mixer, wup_bf, cw, cb, wdn_bf, g, b, *, tm, seq_len):
    x2d, a1, a2, a2_col, wo, g1, b1 = mixer
    rows = x2d.shape[0]
    batch = rows // seq_len
    tps = seq_len // tm
    row_map = lambda bi, t: bi * tps + t
    return pl.pallas_call(
        _ffn_prompt_kernel, grid=(batch, tps),
        in_specs=_mixer_specs(tm, row_map, wo.shape[0] // 2, a2_col, wo.shape) + _ffn_weight_specs(),
        out_specs=[pl.BlockSpec((tm, D_MODEL), lambda bi, t: (bi * tps + t, 0)),
                   pl.BlockSpec((1, CONV_W - 1, D_FF), lambda bi, t: (bi, 0, 0))],
        out_shape=[jax.ShapeDtypeStruct((rows, D_MODEL), F32),
                   jax.ShapeDtypeStruct((batch, CONV_W - 1, D_FF), F32)],
        scratch_shapes=[pltpu.VMEM((tm, D_FF), BF16), pltpu.VMEM((8, D_FF), F32)],
        compiler_params=_cparams(2), name="ffn_prompt",
    )(x2d, a1, a2, wo, g1, b1, wup_bf, cw, cb, wdn_bf, g, b)


def _ffn_sample(mixer, wup_bf, cw, cb, wdn_bf, g, b, conv_state, *, seq_len):
    x2d, a1, a2, a2_col, wo, g1, b1 = mixer
    rows = x2d.shape[0]
    n_seq = rows // seq_len
    zeros = jnp.zeros((n_seq, 1, D_FF), F32)
    inj1 = jnp.concatenate([conv_state[:, 1:2]] + [zeros] * (seq_len - 1), axis=1).reshape(rows, D_FF)
    inj2 = jnp.concatenate([conv_state] + [zeros] * (seq_len - 2), axis=1).reshape(rows, D_FF)
    full = lambda width: pl.BlockSpec((rows, width), lambda i: (0, 0))
    y, gate = pl.pallas_call(
        functools.partial(_ffn_sample_kernel, seq_len=seq_len), grid=(1,),
        in_specs=(_mixer_specs(rows, lambda i: 0, wo.shape[0] // 2, a2_col, wo.shape)
                  + _ffn_weight_specs() + [full(D_FF), full(D_FF)]),
        out_specs=[full(D_MODEL), full(D_FF)],
        out_shape=[jax.ShapeDtypeStruct((rows, D_MODEL), F32), jax.ShapeDtypeStruct((rows, D_FF), F32)],
        scratch_shapes=[pltpu.VMEM((rows, D_FF), BF16)],
        compiler_params=_cparams(1), name="ffn_sample",
    )(x2d, a1, a2, wo, g1, b1, wup_bf, cw, cb, wdn_bf, g, b, inj1, inj2)
    new_state = gate.reshape(n_seq, seq_len, D_FF)[:, seq_len - (CONV_W - 1):]
    return y, new_state


def _ret_kernel(x_ref, w_ref, cos_ref, sin_ref, gg_ref, gb_ref, *refs, decay_len, has_state):
    if has_state:
        s0_ref, y_ref, sout_ref, s_ref = refs
    else:
        y_ref, sout_ref, s_ref = refs
    tm = x_ref.shape[0]
    t = pl.program_id(1)

    @pl.when(t == 0)
    def _():
        if has_state:
            s_ref[...] = s0_ref[0]
        else:
            s_ref[...] = jnp.zeros_like(s_ref)

    x = x_ref[...].astype(BF16)
    cos = cos_ref[...]
    sin = sin_ref[...]
    half = DK_C // 2

    def rotary(a):
        a1, a2 = a[:, :half], a[:, half:]
        return jnp.concatenate([a1 * cos - a2 * sin, a1 * sin + a2 * cos], axis=1)

    ii = lax.broadcasted_iota(jnp.int32, (RET_CHUNK, RET_CHUNK), 0)
    jj = lax.broadcasted_iota(jnp.int32, (RET_CHUNK, RET_CHUNK), 1)
    diff = jnp.where(ii >= jj, ii - jj, 0).astype(F32)
    idx = lax.broadcasted_iota(jnp.int32, (RET_CHUNK, 1), 0).astype(F32)
    idx_tile = (lax.broadcasted_iota(jnp.int32, (tm, 1), 0) % RET_CHUNK).astype(F32)
    v_off = 2 * H_C * DK_C
    g_off = v_off + H_C * DV_C
    n_chunks = tm // RET_CHUNK
    heads = [dict() for _ in range(H_C)]

    def project(h):
        q = rotary(jnp.dot(x, w_ref[:, h * DK_C:(h + 1) * DK_C], preferred_element_type=F32))
        k = rotary(jnp.dot(x, w_ref[:, (H_C + h) * DK_C:(H_C + h + 1) * DK_C],
                           preferred_element_type=F32)) * (DK_C ** -0.5)
        v = jnp.dot(x, w_ref[:, v_off + h * DV_C:v_off + (h + 1) * DV_C], preferred_element_type=F32)
        k_dec = jnp.exp((decay_len - 1.0 - idx_tile) * _LOG_G[h])
        heads[h].update(q=q.astype(BF16), k=k.astype(BF16), v=v.astype(BF16),
                        kd_t=(k * k_dec).T.astype(BF16))

    def project_gate(h):
        heads[h]["gate"] = jnp.dot(x, w_ref[:, g_off + h * DV_C:g_off + (h + 1) * DV_C],
                                   preferred_element_type=F32)

    def retain(h):
        log_g = _LOG_G[h]
        dmat = jnp.where(ii >= jj, jnp.exp(diff * log_g), 0.0)
        q_dec = jnp.exp((idx + 1.0) * log_g)
        c_dec = math.exp(decay_len * log_g)
        q, k, v, kd_t = (heads[h].pop(name) for name in ("q", "k", "v", "kd_t"))
        chunks = [slice(ci * RET_CHUNK, (ci + 1) * RET_CHUNK) for ci in range(n_chunks)]
        kvs = [jnp.dot(kd_t[:, rows], v[rows], preferred_element_type=F32) for rows in chunks]
        states = [s_ref[h]]
        for kv in kvs:
            states.append(states[-1] * c_dec + kv)
        s_ref[h] = states[-1]
        outs = []
        for rows, state in zip(chunks, states):
            qc = q[rows]
            sc = lax.dot_general(qc, k[rows], (((1,), (1,)), ((), ())),
                                 preferred_element_type=F32) * dmat
            outs.append(jnp.dot(sc.astype(BF16), v[rows], preferred_element_type=F32)
                        + jnp.dot(qc, state.astype(BF16), preferred_element_type=F32) * q_dec)
        heads[h]["o"] = outs

    def finish(h):
        vcols = slice(h * DV_C, (h + 1) * DV_C)
        gate = heads[h].pop("gate")
        for ci, o in enumerate(heads[h].pop("o")):
            rows = slice(ci * RET_CHUNK, (ci + 1) * RET_CHUNK)
            mu = jnp.mean(o, axis=-1, keepdims=True)
            oc = o - mu
            var = jnp.mean(oc * oc, axis=-1, keepdims=True)
            normed = oc * lax.rsqrt(var + LN_EPS) * gg_ref[:, vcols] + gb_ref[:, vcols]
            y_ref[rows, vcols] = (jax.nn.silu(gate[rows]) * normed).astype(y_ref.dtype)

    project(0)
    for h in range(H_C):
        if h + 1 < H_C:
            project(h + 1)
        retain(h)
        project_gate(h)
        if h > 0:
            finish(h - 1)
    finish(H_C - 1)

    @pl.when(t == pl.num_programs(1) - 1)
    def _():
        sout_ref[0] = s_ref[...]


def _retention(x2d, w_bf, cos, sin, gg, gb, state0, *, tm, seq_len, decay_len, name):
    rows = x2d.shape[0]
    n_seq = rows // seq_len
    tps = seq_len // tm
    d_y = H_C * DV_C
    has_state = state0 is not None
    state_spec = pl.BlockSpec((1, H_C, DK_C, DV_C), lambda bi, t: (bi, 0, 0, 0))
    in_specs = [pl.BlockSpec((tm, D_MODEL), lambda bi, t: (bi * tps + t, 0)),
                _const_spec(w_bf.shape),
                pl.BlockSpec((tm, DK_C // 2), lambda bi, t: (t, 0)),
                pl.BlockSpec((tm, DK_C // 2), lambda bi, t: (t, 0)),
                _const_spec((1, d_y)), _const_spec((1, d_y))]
    args = [x2d, w_bf, cos, sin, gg, gb]
    if has_state:
        in_specs.append(state_spec)
        args.append(state0)
    return pl.pallas_call(
        functools.partial(_ret_kernel, decay_len=decay_len, has_state=has_state),
        grid=(n_seq, tps), in_specs=in_specs,
        out_specs=[pl.BlockSpec((tm, d_y), lambda bi, t: (bi * tps + t, 0)), state_spec],
        out_shape=[jax.ShapeDtypeStruct((rows, d_y), BF16),
                   jax.ShapeDtypeStruct((n_seq, H_C, DK_C, DV_C), F32)],
        scratch_shapes=[pltpu.VMEM((H_C, DK_C, DV_C), F32)],
        compiler_params=_cparams(2), name=name,
    )(*args)


def _rope_tables(pos):
    half = DK_C // 2
    inv = ROPE_BASE ** (-jnp.arange(half, dtype=F32) / half)
    ang = pos.astype(F32)[:, None] * inv[None, :]
    return jnp.cos(ang), jnp.sin(ang)


def kernel(x_prompt, x_sample, cache_sb_k, cache_sb_v, state_ret, state_ffn_conv, page_table,
           w_in_ab, vln_g, vln_b, w_s, b_s, sb_bias, w_out_ab, w_in_ret, gn_g, gn_b, w_out_ret,
           ln1_g, ln1_b, ln2_g, ln2_b, w_up, conv_w, conv_b, w_down):
    bp, tp, _ = x_prompt.shape
    bs, ts, _ = x_sample.shape
    n_pages = page_table.shape[1]
    past_len = n_pages * PAGE_SIZE
    n_phys = cache_sb_k.shape[1]
    xp = x_prompt.reshape(bp * tp, D_MODEL)
    xs = x_sample.reshape(bs * ts, D_MODEL)
    row2d = lambda a: a.reshape(1, -1)

    sb_k_p, sb_v_p, sb_k_s, sb_v_s, chunk_v_s = [], [], [], [], []
    ret_p, ret_s, conv_p, conv_s = [], [], [], []
    for l in range(DEPTH):
        if l % 2 == 0:
            e = l // 2
            w_in = w_in_ab[e].astype(BF16)
            vg, vb = row2d(vln_g[e]), row2d(vln_b[e])
            mixb_p = jnp.repeat(b_s[e].T, DG_A, axis=1)
            a_p, q_hm, k_hm, v_hm, kf_p, vf_p = _ab_proj(
                xp, w_in, vg, vb, w_s[e], mixb_p, mix_len=CHUNK, head_major=True, tm=512, seq_len=tp)
            bias2 = sb_bias[e] * LOG2E
            b_p = _sb_attn_prompt(q_hm, k_hm, v_hm, bias2, blk=256)

            reps = CHUNK // ts
            wmix_s = jnp.tile(w_s[e][:, :ts, :ts], (1, reps, reps))
            mixb_s = jnp.repeat(jnp.tile(b_s[e][:, :ts], (1, reps)).T, DG_A, axis=1)
            a_s, q_s, kf_s, vf_s, va_s = _ab_proj(
                xs, w_in, vg, vb, wmix_s, mixb_s, mix_len=ts, head_major=False, tm=bs * ts)
            head_of_col = jnp.arange(W_B) // DH_B
            own = (jnp.arange(H_B)[:, None] == head_of_col[None, :]).astype(F32)
            qbd = (q_s.reshape(bs, ts, 1, W_B) * own[None, None]).reshape(bs, ts * H_B, W_B).astype(BF16)
            bias_tile = jnp.broadcast_to(jnp.tile(bias2, ts)[:, None], (ts * H_B, LANES))
            pad = ((0, 0), (0, 0), (0, PAGE_SIZE - ts))
            k_new = jnp.pad(kf_s.reshape(bs, ts, W_B).transpose(0, 2, 1), pad)
            v_new = jnp.pad(vf_s.reshape(bs, ts, W_B).transpose(0, 2, 1), pad)
            pool_k = cache_sb_k[e].transpose(0, 2, 3, 1).reshape(n_phys, W_B, PAGE_SIZE)
            pool_v = cache_sb_v[e].transpose(0, 2, 3, 1).reshape(n_phys, W_B, PAGE_SIZE)
            b_s_out = _sb_decode(page_table, qbd, bias_tile, k_new, v_new, pool_k, pool_v,
                                 pages_per_step=16)

            w_out = w_out_ab[e].astype(BF16)
            g1, b1 = row2d(ln1_g[l]), row2d(ln1_b[l])
            mix_p = (xp, a_p, b_p.reshape(bp * tp, W_B), 0, w_out, g1, b1)
            mix_s = (xs, a_s, b_s_out.reshape(bs * ts, W_B).astype(BF16), 0, w_out, g1, b1)
            pages_of = lambda a: a.reshape(bp, tp // PAGE_SIZE, H_B, DH_B, PAGE_SIZE).transpose(0, 1, 4, 2, 3)
            sb_k_p.append(pages_of(kf_p))
            sb_v_p.append(pages_of(vf_p))
            sb_k_s.append(kf_s.reshape(bs, ts, H_B, DH_B))
            sb_v_s.append(vf_s.reshape(bs, ts, H_B, DH_B))
            chunk_v_s.append(va_s.reshape(bs, ts, G_A, DG_A))
        else:
            o = l // 2
            w_in = w_in_ret[o].astype(BF16)
            gg, gb = row2d(gn_g[o]), row2d(gn_b[o])
            cos_p, sin_p = _rope_tables(jnp.arange(tp))
            y_p, s_p = _retention(xp, w_in, cos_p, sin_p, gg, gb, None, tm=512, seq_len=tp,
                                  decay_len=RET_CHUNK, name="retention_prompt")
            cos_s, sin_s = _rope_tables(past_len + jnp.arange(RET_CHUNK))
            xs_pad = jnp.pad(xs.reshape(bs, ts, D_MODEL), ((0, 0), (0, RET_CHUNK - ts), (0, 0)))
            y_s, s_s = _retention(xs_pad.reshape(bs * RET_CHUNK, D_MODEL), w_in, cos_s, sin_s, gg, gb,
                                  state_ret[o], tm=RET_CHUNK, seq_len=RET_CHUNK, decay_len=ts,
                                  name="retention_sample")
            y_s = y_s.reshape(bs, RET_CHUNK, H_C * DV_C)[:, :ts].reshape(bs * ts, H_C * DV_C)
            w_out = w_out_ret[o].astype(BF16)
            g1, b1 = row2d(ln1_g[l]), row2d(ln1_b[l])
            mix_p = (xp, y_p, y_p, 1, w_out, g1, b1)
            mix_s = (xs, y_s, y_s, 1, w_out, g1, b1)
            ret_p.append(s_p)
            ret_s.append(s_s)

        wup = w_up[l].astype(BF16)
        wdn = w_down[l].astype(BF16)
        g2, b2 = row2d(ln2_g[l]), row2d(ln2_b[l])
        xp, cp = _ffn_prompt(mix_p, wup, conv_w[l], row2d(conv_b[l]), wdn, g2, b2, tm=512, seq_len=tp)
        xs, cs = _ffn_sample(mix_s, wup, conv_w[l], row2d(conv_b[l]), wdn, g2, b2, state_ffn_conv[l], seq_len=ts)
        conv_p.append(cp)
        conv_s.append(cs)

    return (xp.reshape(bp, tp, D_MODEL), xs.reshape(bs, ts, D_MODEL),
            jnp.stack(sb_k_p), jnp.stack(sb_v_p), jnp.stack(sb_k_s), jnp.stack(sb_v_s),
            jnp.stack(chunk_v_s), jnp.stack(ret_p), jnp.stack(ret_s), jnp.stack(conv_p), jnp.stack(conv_s))
```

```python
import functools
import math

import jax
import jax.numpy as jnp
from jax import lax
from jax.experimental import pallas as pl
from jax.experimental.pallas import tpu as pltpu

F32 = jnp.float32
BF16 = jnp.bfloat16

D_MODEL = 1024
W_A = D_MODEL // 2
G_A = 8
DG_A = W_A // G_A
CHUNK = 128
W_B = D_MODEL // 2
H_B = 8
DH_B = W_B // H_B
PAGE_SIZE = 128
H_C = 4
DK_C = D_MODEL // H_C
DV_C = 2 * DK_C
RET_CHUNK = 128
ROPE_BASE = 10000.0
D_FF = ((8 * D_MODEL // 3 + 127) // 128) * 128
CONV_W = 3
DEPTH = 2
ALPHA = (2 * DEPTH) ** 0.25
LN_EPS = 1e-5
LOG2E = 1.4426950408889634
EXP2_CLAMP = 126.0
BIAS_PIECES = 3
DECODE_GROUP = 16

LANES = 128
FF_CHUNK = 256
VMEM_LIMIT = 56 * 1024 * 1024

_LOG_G = tuple(math.log1p(-(2.0 ** (-5.0 - h))) for h in range(H_C))


def _cparams(n_axes):
    return pltpu.CompilerParams(
        dimension_semantics=("arbitrary",) * n_axes, vmem_limit_bytes=VMEM_LIMIT)


def _const_spec(shape):
    nd = len(shape)
    return pl.BlockSpec(shape, lambda *_: (0,) * nd, pipeline_mode=pl.Buffered(1))


def _ln(x, g, b):
    mu = jnp.mean(x, axis=-1, keepdims=True)
    xc = x - mu
    var = jnp.mean(xc * xc, axis=-1, keepdims=True)
    return xc * lax.rsqrt(var + LN_EPS) * g + b


def _softplus2(w):
    return jnp.maximum(w, jnp.log2(1.0 + jnp.exp2(jnp.minimum(w, EXP2_CLAMP))))


def _split_bf16(x):
    hi = x.astype(BF16)
    lo = (x - hi.astype(F32)).astype(BF16)
    return jnp.concatenate([hi, lo], axis=1)


def _emit_skewed(chains):
    depth = max(len(chain) for chain in chains)
    for t in range(len(chains) + depth - 1):
        for s in range(depth):
            if 0 <= t - s < len(chains) and s < len(chains[t - s]):
                chains[t - s][s]()


def _neg_suffix_matrix(n, parts, inclusive):
    j = lax.broadcasted_iota(jnp.int32, (parts * n, n), 0) % n
    s = lax.broadcasted_iota(jnp.int32, (parts * n, n), 1)
    return jnp.where((j >= s) if inclusive else (j > s), -1.0, 0.0).astype(BF16)


def _ab_proj_kernel(x_ref, w_ref, vg_ref, vb_ref, wmix_ref, mixb_ref, *refs, mix_len, head_major):
    if head_major:
        a_ref, q_ref, k_ref, v_ref, kf_ref, vf_ref, mixed_ref = refs
    else:
        a_ref, q_ref, kf_ref, vf_ref, va_ref, mixed_ref = refs
    tm = x_ref.shape[0]
    x = x_ref[...].astype(BF16)
    h = jnp.dot(x, w_ref[...], preferred_element_type=F32)
    u = jax.nn.gelu(h[:, :W_A])
    va = _ln(jax.nn.gelu(h[:, W_A:2 * W_A]), vg_ref[...], vb_ref[...])
    q = h[:, 2 * W_A:2 * W_A + W_B] * (DH_B ** -0.5 * LOG2E)
    k = h[:, 2 * W_A + W_B:2 * W_A + 2 * W_B]
    v = h[:, 2 * W_A + 2 * W_B:]
    if head_major:
        for p in range(W_B // LANES):
            sl = slice(p * LANES, (p + 1) * LANES)
            q_ref[0, p] = q[:, sl].astype(BF16)
            k_ref[0, p] = k[:, sl].astype(BF16)
            v_ref[0, p] = v[:, sl].astype(BF16)
        for pg in range(tm // PAGE_SIZE):
            rows = slice(pg * PAGE_SIZE, (pg + 1) * PAGE_SIZE)
            kf_ref[pg] = k[rows].T
            vf_ref[pg] = v[rows].T
    else:
        kf_ref[...] = k
        vf_ref[...] = v
        q_ref[...] = q
        va_ref[...] = va

    r = lax.broadcasted_iota(jnp.int32, (CHUNK, CHUNK), 0)
    c = lax.broadcasted_iota(jnp.int32, (CHUNK, CHUNK), 1)
    if mix_len == CHUNK:
        mask = c <= r
    else:
        mask = jnp.logical_and(r // mix_len == c // mix_len, c % mix_len <= r % mix_len)
    wg = [jnp.where(mask, wmix_ref[g], 0.0).astype(BF16) for g in range(G_A)]
    for ci in range(tm // CHUNK):
        rows = slice(ci * CHUNK, (ci + 1) * CHUNK)
        va_c = va[rows]
        for g in range(G_A):
            cols = slice(g * DG_A, (g + 1) * DG_A)
            mixed_ref[:, cols] = jnp.dot(wg[g], va_c[:, cols].astype(BF16), preferred_element_type=F32)
        a_ref[rows, :] = (u[rows] * (mixed_ref[...] + mixb_ref[...])).astype(BF16)


def _ab_proj(x2d, w_bf, vg, vb, wmix, mixb, *, mix_len, head_major, tm, seq_len=None):
    rows = x2d.shape[0]
    d_in = w_bf.shape[1]
    n_tiles = rows // tm
    row_spec = lambda width: pl.BlockSpec((tm, width), lambda i: (i, 0))
    in_specs = [row_spec(D_MODEL), _const_spec((D_MODEL, d_in)), _const_spec((1, W_A)),
                _const_spec((1, W_A)), _const_spec((G_A, CHUNK, CHUNK)), _const_spec((CHUNK, W_A))]
    if head_major:
        tps = seq_len // tm
        batch = rows // seq_len
        hm_spec = pl.BlockSpec((1, W_B // LANES, tm, LANES), lambda i: (i // tps, 0, i % tps, 0))
        hm_shape = jax.ShapeDtypeStruct((batch, W_B // LANES, seq_len, LANES), BF16)
        ppt = tm // PAGE_SIZE
        pg_spec = pl.BlockSpec((ppt, W_B, PAGE_SIZE), lambda i: (i, 0, 0))
        pg_shape = jax.ShapeDtypeStruct((rows // PAGE_SIZE, W_B, PAGE_SIZE), F32)
        out_specs = [row_spec(W_A), hm_spec, hm_spec, hm_spec, pg_spec, pg_spec]
        out_shape = [jax.ShapeDtypeStruct((rows, W_A), BF16), hm_shape, hm_shape, hm_shape,
                     pg_shape, pg_shape]
    else:
        out_specs = [row_spec(W_A), row_spec(W_B), row_spec(W_B), row_spec(W_B), row_spec(W_A)]
        out_shape = [jax.ShapeDtypeStruct((rows, W_A), BF16)] + [
            jax.ShapeDtypeStruct((rows, W_B), F32)] * 3 + [jax.ShapeDtypeStruct((rows, W_A), F32)]
    return pl.pallas_call(
        functools.partial(_ab_proj_kernel, mix_len=mix_len, head_major=head_major),
        grid=(n_tiles,), in_specs=in_specs, out_specs=out_specs, out_shape=out_shape,
        scratch_shapes=[pltpu.VMEM((CHUNK, W_A), F32)],
        compiler_params=_cparams(1),
        name="ab_proj_prompt" if head_major else "ab_proj_sample",
    )(x2d, w_bf, vg, vb, wmix, mixb)


def _attn_step(qi, bias_ref, q_ref, k_ref, v_ref, o_ref, acc_ref, carry_ref, blk, extra_chains):
    n_pairs = q_ref.shape[1]
    n_heads = 2 * n_pairs
    lane = lax.broadcasted_iota(jnp.int32, (blk, LANES), 1)
    neg_u = _neg_suffix_matrix(blk, 1, inclusive=False)
    row = lax.broadcasted_iota(jnp.int32, (blk, blk), 0)
    col = lax.broadcasted_iota(jnp.int32, (blk, blk), 1)
    causal = col < row
    k_ones = jnp.where(lane < BIAS_PIECES, 1.0, 0.0).astype(BF16)
    q_aug = []
    for h in range(n_heads):
        q2 = q_ref[0, h // 2]
        qh = jnp.where((lane >= DH_B) == bool(h % 2), q2, jnp.zeros_like(q2))
        q_aug.append(jnp.concatenate(
            [qh, jnp.broadcast_to(bias_ref[h:h + 1, :], (blk, LANES))], axis=1))

    def start_of(kb):
        return kb * blk if isinstance(kb, int) else pl.multiple_of(kb * blk, blk)

    def block_chains(kb, diagonal):
        start = start_of(kb)
        vals = [dict() for _ in range(n_heads)]

        def logits(h):
            k2 = k_ref[0, h // 2, pl.ds(start, blk), :]
            vals[h]["w"] = lax.dot_general(q_aug[h], jnp.concatenate([k2, k_ones], axis=1),
                                           (((1,), (1,)), ((), ())), preferred_element_type=F32)

        def keep(h):
            w = vals[h].pop("w")
            sp = _softplus2(w)
            if diagonal:
                sp = jnp.where(causal, sp, 0.0)
                vals[h]["d"] = w - sp
                carry_ref[h] = jnp.broadcast_to(-jnp.sum(sp, axis=1, keepdims=True), (blk, LANES))
            else:
                carry = carry_ref[h]
                vals[h]["d"] = w - sp + jnp.concatenate([carry] * (blk // LANES), axis=1)
                carry_ref[h] = carry - jnp.sum(sp, axis=1, keepdims=True)
            vals[h]["sp"] = sp.astype(BF16)

        def suffix(h):
            vals[h]["suf"] = jnp.dot(vals[h].pop("sp"), neg_u, preferred_element_type=F32)

        def weights(h):
            a = jnp.exp2(vals[h].pop("d") + vals[h].pop("suf"))
            if diagonal:
                a = jnp.where(causal, a, 0.0)
            vals[h]["a"] = a.astype(BF16)

        def values(h):
            v2 = v_ref[0, h // 2, pl.ds(start, blk), :]
            pv = jnp.dot(vals[h].pop("a"), v2, preferred_element_type=F32)
            if diagonal:
                acc_ref[h] = pv
            else:
                acc_ref[h] += pv

        return [[functools.partial(stage, h) for stage in (logits, keep, suffix, weights, values)]
                for h in range(n_heads)]

    def blocks(kbs, diagonal):
        _emit_skewed([chain for kb in kbs for chain in block_chains(kb, diagonal)])

    diag = block_chains(qi, True)
    every = max(1, len(diag) // max(1, len(extra_chains)))
    merged = []
    for i, chain in enumerate(diag):
        if i % every == 0 and i // every < len(extra_chains):
            merged.append(extra_chains[i // every])
        merged.append(chain)
    _emit_skewed(merged)

    def pair(j, _):
        blocks([qi - 1 - 2 * j, qi - 2 - 2 * j], False)
        return 0

    lax.fori_loop(0, qi // 2, pair, 0)

    @pl.when(qi % 2 == 1)
    def _():
        blocks([0], False)

    for hp in range(n_pairs):
        o_ref[0, :, hp * LANES:(hp + 1) * LANES] = jnp.where(
            lane < DH_B, acc_ref[2 * hp], acc_ref[2 * hp + 1]).astype(o_ref.dtype)


def _bias_rows(bias2):
    pieces, rest = [], bias2.astype(F32)
    for _ in range(BIAS_PIECES):
        piece = rest.astype(BF16)
        pieces.append(piece)
        rest = rest - piece.astype(F32)
    rows = jnp.stack(pieces, axis=1)
    return jnp.pad(rows, ((0, 0), (0, LANES - BIAS_PIECES)))


def _decode_chains(qbd_ref, bias_ref, k_pgs, v_pgs, acc_ref, carry_ref, valid=None):
    n_rows = qbd_ref.shape[1]
    neg_u = _neg_suffix_matrix(PAGE_SIZE, 2, inclusive=True)

    def chain(k_grp, v_grp):
        n = len(k_grp)
        vals = {}

        def logits():
            k_cat = jnp.concatenate([k[0].astype(BF16) for k in k_grp], axis=1)
            vals["w_cat"] = jnp.dot(qbd_ref[0], k_cat, preferred_element_type=F32)

        def keep():
            w_cat = vals.pop("w_cat")
            w = jnp.concatenate([w_cat[:, p * PAGE_SIZE:(p + 1) * PAGE_SIZE] for p in range(n)], axis=0)
            w = w + jnp.concatenate([bias_ref[...]] * n, axis=0)
            sp = _softplus2(w)
            if valid is not None:
                sp = jnp.where(valid, sp, 0.0)
            tot = jnp.sum(sp, axis=1, keepdims=True)
            carry = carry_ref[...]
            carries = []
            for p in range(n):
                carries.append(carry)
                carry = carry - tot[p * n_rows:(p + 1) * n_rows]
            carry_ref[...] = carry
            vals["d"] = w + jnp.concatenate(carries, axis=0)
            vals["split"] = _split_bf16(sp)

        def suffix():
            vals["suf"] = jnp.dot(vals.pop("split"), neg_u, preferred_element_type=F32)

        def weights():
            a = jnp.exp2(vals.pop("d") + vals.pop("suf"))
            if valid is not None:
                a = jnp.where(valid, a, 0.0)
            a = a.astype(BF16)
            vals["a_cat"] = jnp.concatenate([a[p * n_rows:(p + 1) * n_rows] for p in range(n)], axis=1)

        def values():
            v_cat = jnp.concatenate([v[0].astype(BF16) for v in v_grp], axis=1)
            acc_ref[...] += lax.dot_general(vals.pop("a_cat"), v_cat, (((1,), (1,)), ((), ())),
                                            preferred_element_type=F32)

        return [logits, keep, suffix, weights, values]

    return [chain(k_pgs[first:first + DECODE_GROUP], v_pgs[first:first + DECODE_GROUP])
            for first in range(0, len(k_pgs), DECODE_GROUP)]


def _decode_begin(j, qbd_ref, bias_ref, knew_ref, vnew_ref, acc_ref, carry_ref):
    n_rows = qbd_ref.shape[1]

    @pl.when(j == 0)
    def _():
        acc_ref[...] = jnp.zeros_like(acc_ref)
        carry_ref[...] = jnp.zeros_like(carry_ref)
        t = lax.broadcasted_iota(jnp.int32, (n_rows, PAGE_SIZE), 0) // H_B
        i = lax.broadcasted_iota(jnp.int32, (n_rows, PAGE_SIZE), 1)
        _emit_skewed(_decode_chains(qbd_ref, bias_ref, [knew_ref], [vnew_ref], acc_ref, carry_ref,
                                    valid=i < t))


def _decode_end(j, n_steps, o_ref, acc_ref, n_q):
    n_rows = acc_ref.shape[0]

    @pl.when(j == n_steps - 1)
    def _():
        rr = lax.broadcasted_iota(jnp.int32, (n_rows, W_B), 0) % H_B
        cc = lax.broadcasted_iota(jnp.int32, (n_rows, W_B), 1) // DH_B
        own = jnp.where(rr == cc, acc_ref[...], 0.0)
        o_ref[0] = jnp.sum(own.reshape(n_q, H_B, W_B), axis=1)


def _sb_kernel(pt_ref, bias_ref, q_ref, k_ref, v_ref, qbd_ref, dbias_ref, knew_ref, vnew_ref,
               *refs, blk, pages_per_step, steps_per_seq, n_q):
    del pt_ref
    k_refs = refs[:pages_per_step]
    v_refs = refs[pages_per_step:2 * pages_per_step]
    o_ref, od_ref, acc_ref, carry_ref, dacc_ref, dcarry_ref = refs[2 * pages_per_step:]
    j = (pl.program_id(0) * pl.num_programs(1) + pl.program_id(1)) % steps_per_seq
    _decode_begin(j, qbd_ref, dbias_ref, knew_ref, vnew_ref, dacc_ref, dcarry_ref)
    decode = _decode_chains(qbd_ref, dbias_ref, k_refs, v_refs, dacc_ref, dcarry_ref)
    _attn_step(pl.program_id(1), bias_ref, q_ref, k_ref, v_ref, o_ref, acc_ref, carry_ref, blk, decode)
    _decode_end(j, steps_per_seq, od_ref, dacc_ref, n_q)


def _sb_attention(q_hm, k_hm, v_hm, bias2, page_table, qbd, bias_tile, k_new, v_new, pool_k, pool_v,
                  *, blk):
    batch, n_pairs, seq_len, _ = q_hm.shape
    nq = seq_len // blk
    n_heads = 2 * n_pairs
    n_seq, n_pages = page_table.shape
    n_rows = qbd.shape[1]
    n_q = n_rows // H_B
    n_steps = batch * nq
    pages_per_step = n_seq * n_pages // n_steps
    steps_per_seq = n_pages // pages_per_step
    assert pages_per_step * n_steps == n_seq * n_pages and steps_per_seq * pages_per_step == n_pages
    assert pages_per_step % DECODE_GROUP == 0

    seq_of = lambda b, i: (b * nq + i) // steps_per_seq

    def page_spec(p):
        def index(b, i, pt):
            j = (b * nq + i) % steps_per_seq
            return (pt[seq_of(b, i), n_pages - 1 - (j * pages_per_step + p)], 0, 0)
        return pl.BlockSpec((1, W_B, PAGE_SIZE), index)

    per_seq = lambda shape: pl.BlockSpec((1,) + shape, lambda b, i, pt: (seq_of(b, i), 0, 0))
    kv_spec = pl.BlockSpec((1, n_pairs, seq_len, LANES), lambda b, i, pt: (b, 0, 0, 0),
                           pipeline_mode=pl.Buffered(1))
    grid_spec = pltpu.PrefetchScalarGridSpec(
        num_scalar_prefetch=1, grid=(batch, nq),
        in_specs=[pl.BlockSpec((n_heads, LANES), lambda b, i, pt: (0, 0)),
                  pl.BlockSpec((1, n_pairs, blk, LANES), lambda b, i, pt: (b, 0, i, 0)),
                  kv_spec, kv_spec,
                  per_seq((n_rows, W_B)),
                  pl.BlockSpec((n_rows, LANES), lambda b, i, pt: (0, 0)),
                  per_seq((W_B, PAGE_SIZE)), per_seq((W_B, PAGE_SIZE))]
                 + [page_spec(p) for p in range(pages_per_step)] * 2,
        out_specs=[pl.BlockSpec((1, blk, n_pairs * LANES), lambda b, i, pt: (b, i, 0)),
                   per_seq((n_q, W_B))],
        scratch_shapes=[pltpu.VMEM((n_heads, blk, LANES), F32), pltpu.VMEM((n_heads, blk, LANES), F32),
                        pltpu.VMEM((n_rows, W_B), F32), pltpu.VMEM((n_rows, LANES), F32)])
    return pl.pallas_call(
        functools.partial(_sb_kernel, blk=blk, pages_per_step=pages_per_step,
                          steps_per_seq=steps_per_seq, n_q=n_q),
        grid_spec=grid_spec,
        out_shape=[jax.ShapeDtypeStruct((batch, seq_len, n_pairs * LANES), BF16),
                   jax.ShapeDtypeStruct((n_seq, n_q, W_B), F32)],
        compiler_params=_cparams(2), name="sb_attention",
    )(page_table, _bias_rows(bias2), q_hm, k_hm, v_hm, qbd, bias_tile, k_new, v_new,
      *([pool_k] * pages_per_step), *([pool_v] * pages_per_step))


def _mixer_out_ln(x_ref, a1_ref, a2_ref, wo_ref, g_ref, b_ref):
    a = jnp.concatenate([a1_ref[...], a2_ref[...]], axis=1)
    y = jnp.dot(a, wo_ref[...], preferred_element_type=F32)
    return _ln(ALPHA * x_ref[...] + y, g_ref[...], b_ref[...])


def _mixer_specs(tm, row_map, k1, a2_block_col, w_shape):
    i0 = lambda *idx: (row_map(*idx), 0)
    i2 = lambda *idx: (row_map(*idx), a2_block_col)
    return [pl.BlockSpec((tm, D_MODEL), i0), pl.BlockSpec((tm, k1), i0), pl.BlockSpec((tm, k1), i2),
            _const_spec(w_shape), _const_spec((1, D_MODEL)), _const_spec((1, D_MODEL))]


def _ffn_chunk(x_bf, wup_ref, cw_ref, cb_ref, ci, shift_fix):
    cols = slice(ci * FF_CHUNK, (ci + 1) * FF_CHUNK)
    vcols = slice(D_FF + ci * FF_CHUNK, D_FF + (ci + 1) * FF_CHUNK)
    g = jnp.dot(x_bf, wup_ref[:, cols], preferred_element_type=F32)
    val = jnp.dot(x_bf, wup_ref[:, vcols], preferred_element_type=F32)
    g1, g2 = shift_fix(g, cols)
    cw = cw_ref[:, cols]
    conv = cb_ref[:, cols] + cw[0:1] * g2 + cw[1:2] * g1 + cw[2:3] * g
    return g, (jax.nn.gelu(conv) * val).astype(BF16)


def _ffn_prompt_kernel(x_ref, a1_ref, a2_ref, wo_ref, g1_ref, b1_ref,
                       wup_ref, cw_ref, cb_ref, wdn_ref, g_ref, b_ref,
                       y_ref, conv_ref, act_ref, prev_ref):
    tm = x_ref.shape[0]

    @pl.when(pl.program_id(1) == 0)
    def _():
        prev_ref[...] = jnp.zeros_like(prev_ref)

    x = _mixer_out_ln(x_ref, a1_ref, a2_ref, wo_ref, g1_ref, b1_ref)
    x_bf = x.astype(BF16)
    row = lax.broadcasted_iota(jnp.int32, (tm, FF_CHUNK), 0)

    def shift_fix(g, cols):
        p0 = prev_ref[0:1, cols]
        p1 = prev_ref[1:2, cols]
        g1 = jnp.where(row == 0, p1, pltpu.roll(g, 1, 0))
        g2 = jnp.where(row == 0, p0, jnp.where(row == 1, p1, pltpu.roll(g, 2, 0)))
        return g1, g2

    for ci in range(D_FF // FF_CHUNK):
        cols = slice(ci * FF_CHUNK, (ci + 1) * FF_CHUNK)
        g, act = _ffn_chunk(x_bf, wup_ref, cw_ref, cb_ref, ci, shift_fix)
        act_ref[:, cols] = act
        prev_ref[0:2, cols] = g[tm - 2:tm]
        conv_ref[0, :, cols] = g[tm - 2:tm]
    y = jnp.dot(act_ref[...], wdn_ref[...], preferred_element_type=F32)
    y_ref[...] = _ln(ALPHA * x + y, g_ref[...], b_ref[...])


def _ffn_sample_kernel(x_ref, a1_ref, a2_ref, wo_ref, g1_ref, b1_ref,
                       wup_ref, cw_ref, cb_ref, wdn_ref, g_ref, b_ref, inj1_ref, inj2_ref,
                       y_ref, gate_ref, act_ref, *, seq_len):
    tm = x_ref.shape[0]
    x = _mixer_out_ln(x_ref, a1_ref, a2_ref, wo_ref, g1_ref, b1_ref)
    x_bf = x.astype(BF16)
    pos = lax.broadcasted_iota(jnp.int32, (tm, FF_CHUNK), 0) % seq_len

    def shift_fix(g, cols):
        g1 = jnp.where(pos >= 1, pltpu.roll(g, 1, 0), inj1_ref[:, cols])
        g2 = jnp.where(pos >= 2, pltpu.roll(g, 2, 0), inj2_ref[:, cols])
        return g1, g2

    for ci in range(D_FF // FF_CHUNK):
        cols = slice(ci * FF_CHUNK, (ci + 1) * FF_CHUNK)
        g, act = _ffn_chunk(x_bf, wup_ref, cw_ref, cb_ref, ci, shift_fix)
        act_ref[:, cols] = act
        gate_ref[:, cols] = g
    y = jnp.dot(act_ref[...], wdn_ref[...], preferred_element_type=F32)
    y_ref[...] = _ln(ALPHA * x + y, g_ref[...], b_ref[...])


def _ffn_weight_specs():
    return [_const_spec((D_MODEL, 2 * D_FF)), _const_spec((CONV_W, D_FF)), _const_spec((1, D_FF)),
            _const_spec((D_FF, D_MODEL)), _const_spec((1, D_MODEL)), _const_spec((1, D_MODEL))]


def _ffn_prompt(mixer, wup_bf, cw, cb, wdn_bf, g, b, *, tm, seq_len):
    x2d, a1, a2, a2_col, wo, g1, b1 = mixer
    rows = x2d.shape[0]
    batch = rows // seq_len
    tps = seq_len // tm
    row_map = lambda bi, t: bi * tps + t
    return pl.pallas_call(
        _ffn_prompt_kernel, grid=(batch, tps),
        in_specs=_mixer_specs(tm, row_map, wo.shape[0] // 2, a2_col, wo.shape) + _ffn_weight_specs(),
        out_specs=[pl.BlockSpec((tm, D_MODEL), lambda bi, t: (bi * tps + t, 0)),
                   pl.BlockSpec((1, CONV_W - 1, D_FF), lambda bi, t: (bi, 0, 0))],
        out_shape=[jax.ShapeDtypeStruct((rows, D_MODEL), F32),
                   jax.ShapeDtypeStruct((batch, CONV_W - 1, D_FF), F32)],
        scratch_shapes=[pltpu.VMEM((tm, D_FF), BF16), pltpu.VMEM((8, D_FF), F32)],
        compiler_params=_cparams(2), name="ffn_prompt",
    )(x2d, a1, a2, wo, g1, b1, wup_bf, cw, cb, wdn_bf, g, b)


def _ffn_sample(mixer, wup_bf, cw, cb, wdn_bf, g, b, conv_state, *, seq_len):
    x2d, a1, a2, a2_col, wo, g1, b1 = mixer
    rows = x2d.shape[0]
    n_seq = rows // seq_len
    zeros = jnp.zeros((n_seq, 1, D_FF), F32)
    inj1 = jnp.concatenate([conv_state[:, 1:2]] + [zeros] * (seq_len - 1), axis=1).reshape(rows, D_FF)
    inj2 = jnp.concatenate([conv_state] + [zeros] * (seq_len - 2), axis=1).reshape(rows, D_FF)
    full = lambda width: pl.BlockSpec((rows, width), lambda i: (0, 0))
    y, gate = pl.pallas_call(
        functools.partial(_ffn_sample_kernel, seq_len=seq_len), grid=(1,),
        in_specs=(_mixer_specs(rows, lambda i: 0, wo.shape[0] // 2, a2_col, wo.shape)
                  + _ffn_weight_specs() + [full(D_FF), full(D_FF)]),
        out_specs=[full(D_MODEL), full(D_FF)],
        out_shape=[jax.ShapeDtypeStruct((rows, D_MODEL), F32), jax.ShapeDtypeStruct((rows, D_FF), F32)],
        scratch_shapes=[pltpu.VMEM((rows, D_FF), BF16)],
        compiler_params=_cparams(1), name="ffn_sample",
    )(x2d, a1, a2, wo, g1, b1, wup_bf, cw, cb, wdn_bf, g, b, inj1, inj2)
    new_state = gate.reshape(n_seq, seq_len, D_FF)[:, seq_len - (CONV_W - 1):]
    return y, new_state


def _ret_kernel(x_ref, w_ref, cos_ref, sin_ref, gg_ref, gb_ref, *refs, decay_len, has_state):
    if has_state:
        s0_ref, y_ref, sout_ref, s_ref = refs
    else:
        y_ref, sout_ref, s_ref = refs
    tm = x_ref.shape[0]
    t = pl.program_id(1)

    @pl.when(t == 0)
    def _():
        if has_state:
            s_ref[...] = s0_ref[0]
        else:
            s_ref[...] = jnp.zeros_like(s_ref)

    x = x_ref[...].astype(BF16)
    cos = cos_ref[...]
    sin = sin_ref[...]
    half = DK_C // 2

    def rotary(a):
        a1, a2 = a[:, :half], a[:, half:]
        return jnp.concatenate([a1 * cos - a2 * sin, a1 * sin + a2 * cos], axis=1)

    ii = lax.broadcasted_iota(jnp.int32, (RET_CHUNK, RET_CHUNK), 0)
    jj = lax.broadcasted_iota(jnp.int32, (RET_CHUNK, RET_CHUNK), 1)
    diff = jnp.where(ii >= jj, ii - jj, 0).astype(F32)
    idx = lax.broadcasted_iota(jnp.int32, (RET_CHUNK, 1), 0).astype(F32)
    idx_tile = (lax.broadcasted_iota(jnp.int32, (tm, 1), 0) % RET_CHUNK).astype(F32)
    v_off = 2 * H_C * DK_C
    g_off = v_off + H_C * DV_C
    n_chunks = tm // RET_CHUNK
    heads = [dict() for _ in range(H_C)]

    def project(h):
        q = rotary(jnp.dot(x, w_ref[:, h * DK_C:(h + 1) * DK_C], preferred_element_type=F32))
        k = rotary(jnp.dot(x, w_ref[:, (H_C + h) * DK_C:(H_C + h + 1) * DK_C],
                           preferred_element_type=F32)) * (DK_C ** -0.5)
        v = jnp.dot(x, w_ref[:, v_off + h * DV_C:v_off + (h + 1) * DV_C], preferred_element_type=F32)
        k_dec = jnp.exp((decay_len - 1.0 - idx_tile) * _LOG_G[h])
        heads[h].update(q=q.astype(BF16), k=k.astype(BF16), v=v.astype(BF16),
                        kd_t=(k * k_dec).T.astype(BF16))

    def project_gate(h):
        heads[h]["gate"] = jnp.dot(x, w_ref[:, g_off + h * DV_C:g_off + (h + 1) * DV_C],
                                   preferred_element_type=F32)

    def retain(h):
        log_g = _LOG_G[h]
        dmat = jnp.where(ii >= jj, jnp.exp(diff * log_g), 0.0)
        q_dec = jnp.exp((idx + 1.0) * log_g)
        c_dec = math.exp(decay_len * log_g)
        q, k, v, kd_t = (heads[h].pop(name) for name in ("q", "k", "v", "kd_t"))
        chunks = [slice(ci * RET_CHUNK, (ci + 1) * RET_CHUNK) for ci in range(n_chunks)]
        kvs = [jnp.dot(kd_t[:, rows], v[rows], preferred_element_type=F32) for rows in chunks]
        states = [s_ref[h]]
        for kv in kvs:
            states.append(states[-1] * c_dec + kv)
        s_ref[h] = states[-1]
        outs = []
        for rows, state in zip(chunks, states):
            qc = q[rows]
            sc = lax.dot_general(qc, k[rows], (((1,), (1,)), ((), ())),
                                 preferred_element_type=F32) * dmat
            outs.append(jnp.dot(sc.astype(BF16), v[rows], preferred_element_type=F32)
                        + jnp.dot(qc, state.astype(BF16), preferred_element_type=F32) * q_dec)
        heads[h]["o"] = outs

    def finish(h):
        vcols = slice(h * DV_C, (h + 1) * DV_C)
        gate = heads[h].pop("gate")
        for ci, o in enumerate(heads[h].pop("o")):
            rows = slice(ci * RET_CHUNK, (ci + 1) * RET_CHUNK)
            mu = jnp.mean(o, axis=-1, keepdims=True)
            oc = o - mu
            var = jnp.mean(oc * oc, axis=-1, keepdims=True)
            normed = oc * lax.rsqrt(var + LN_EPS) * gg_ref[:, vcols] + gb_ref[:, vcols]
            y_ref[rows, vcols] = (jax.nn.silu(gate[rows]) * normed).astype(y_ref.dtype)

    project(0)
    for h in range(H_C):
        if h + 1 < H_C:
            project(h + 1)
        retain(h)
        project_gate(h)
        if h > 0:
            finish(h - 1)
    finish(H_C - 1)

    @pl.when(t == pl.num_programs(1) - 1)
    def _():
        sout_ref[0] = s_ref[...]


def _retention(x2d, w_bf, cos, sin, gg, gb, state0, *, tm, seq_len, decay_len, name):
    rows = x2d.shape[0]
    n_seq = rows // seq_len
    tps = seq_len // tm
    d_y = H_C * DV_C
    has_state = state0 is not None
    state_spec = pl.BlockSpec((1, H_C, DK_C, DV_C), lambda bi, t: (bi, 0, 0, 0))
    in_specs = [pl.BlockSpec((tm, D_MODEL), lambda bi, t: (bi * tps + t, 0)),
                _const_spec(w_bf.shape),
                pl.BlockSpec((tm, DK_C // 2), lambda bi, t: (t, 0)),
                pl.BlockSpec((tm, DK_C // 2), lambda bi, t: (t, 0)),
                _const_spec((1, d_y)), _const_spec((1, d_y))]
    args = [x2d, w_bf, cos, sin, gg, gb]
    if has_state:
        in_specs.append(state_spec)
        args.append(state0)
    return pl.pallas_call(
        functools.partial(_ret_kernel, decay_len=decay_len, has_state=has_state),
        grid=(n_seq, tps), in_specs=in_specs,
        out_specs=[pl.BlockSpec((tm, d_y), lambda bi, t: (bi * tps + t, 0)), state_spec],
        out_shape=[jax.ShapeDtypeStruct((rows, d_y), BF16),
                   jax.ShapeDtypeStruct((n_seq, H_C, DK_C, DV_C), F32)],
        scratch_shapes=[pltpu.VMEM((H_C, DK_C, DV_C), F32)],
        compiler_params=_cparams(2), name=name,
    )(*args)


def _rope_tables(pos):
    half = DK_C // 2
    inv = ROPE_BASE ** (-jnp.arange(half, dtype=F32) / half)
    ang = pos.astype(F32)[:, None] * inv[None, :]
    return jnp.cos(ang), jnp.sin(ang)


def kernel(x_prompt, x_sample, cache_sb_k, cache_sb_v, state_ret, state_ffn_conv, page_table,
           w_in_ab, vln_g, vln_b, w_s, b_s, sb_bias, w_out_ab, w_in_ret, gn_g, gn_b, w_out_ret,
           ln1_g, ln1_b, ln2_g, ln2_b, w_up, conv_w, conv_b, w_down):
    bp, tp, _ = x_prompt.shape
    bs, ts, _ = x_sample.shape
    n_pages = page_table.shape[1]
    past_len = n_pages * PAGE_SIZE
    n_phys = cache_sb_k.shape[1]
    xp = x_prompt.reshape(bp * tp, D_MODEL)
    xs = x_sample.reshape(bs * ts, D_MODEL)
    row2d = lambda a: a.reshape(1, -1)

    sb_k_p, sb_v_p, sb_k_s, sb_v_s, chunk_v_s = [], [], [], [], []
    ret_p, ret_s, conv_p, conv_s = [], [], [], []
    for l in range(DEPTH):
        if l % 2 == 0:
            e = l // 2
            w_in = w_in_ab[e].astype(BF16)
            vg, vb = row2d(vln_g[e]), row2d(vln_b[e])
            mixb_p = jnp.repeat(b_s[e].T, DG_A, axis=1)
            a_p, q_hm, k_hm, v_hm, kf_p, vf_p = _ab_proj(
                xp, w_in, vg, vb, w_s[e], mixb_p, mix_len=CHUNK, head_major=True, tm=512, seq_len=tp)
            bias2 = sb_bias[e] * LOG2E

            reps = CHUNK // ts
            wmix_s = jnp.tile(w_s[e][:, :ts, :ts], (1, reps, reps))
            mixb_s = jnp.repeat(jnp.tile(b_s[e][:, :ts], (1, reps)).T, DG_A, axis=1)
            a_s, q_s, kf_s, vf_s, va_s = _ab_proj(
                xs, w_in, vg, vb, wmix_s, mixb_s, mix_len=ts, head_major=False, tm=bs * ts)
            head_of_col = jnp.arange(W_B) // DH_B
            own = (jnp.arange(H_B)[:, None] == head_of_col[None, :]).astype(F32)
            qbd = (q_s.reshape(bs, ts, 1, W_B) * own[None, None]).reshape(bs, ts * H_B, W_B).astype(BF16)
            bias_tile = jnp.broadcast_to(jnp.tile(bias2, ts)[:, None], (ts * H_B, LANES))
            pad = ((0, 0), (0, 0), (0, PAGE_SIZE - ts))
            k_new = jnp.pad(kf_s.reshape(bs, ts, W_B).transpose(0, 2, 1), pad)
            v_new = jnp.pad(vf_s.reshape(bs, ts, W_B).transpose(0, 2, 1), pad)
            pool_k = cache_sb_k[e].transpose(0, 2, 3, 1).reshape(n_phys, W_B, PAGE_SIZE)
            pool_v = cache_sb_v[e].transpose(0, 2, 3, 1).reshape(n_phys, W_B, PAGE_SIZE)
            b_p, b_s_out = _sb_attention(q_hm, k_hm, v_hm, bias2, page_table, qbd, bias_tile,
                                         k_new, v_new, pool_k, pool_v, blk=256)

            w_out = w_out_ab[e].astype(BF16)
            g1, b1 = row2d(ln1_g[l]), row2d(ln1_b[l])
            mix_p = (xp, a_p, b_p.reshape(bp * tp, W_B), 0, w_out, g1, b1)
            mix_s = (xs, a_s, b_s_out.reshape(bs * ts, W_B).astype(BF16), 0, w_out, g1, b1)
            pages_of = lambda a: a.reshape(bp, tp // PAGE_SIZE, H_B, DH_B, PAGE_SIZE).transpose(0, 1, 4, 2, 3)
            sb_k_p.append(pages_of(kf_p))
            sb_v_p.append(pages_of(vf_p))
            sb_k_s.append(kf_s.reshape(bs, ts, H_B, DH_B))
            sb_v_s.append(vf_s.reshape(bs, ts, H_B, DH_B))
            chunk_v_s.append(va_s.reshape(bs, ts, G_A, DG_A))
        else:
            o = l // 2
            w_in = w_in_ret[o].astype(BF16)
            gg, gb = row2d(gn_g[o]), row2d(gn_b[o])
            cos_p, sin_p = _rope_tables(jnp.arange(tp))
            y_p, s_p = _retention(xp, w_in, cos_p, sin_p, gg, gb, None, tm=512, seq_len=tp,
                                  decay_len=RET_CHUNK, name="retention_prompt")
            cos_s, sin_s = _rope_tables(past_len + jnp.arange(RET_CHUNK))
            xs_pad = jnp.pad(xs.reshape(bs, ts, D_MODEL), ((0, 0), (0, RET_CHUNK - ts), (0, 0)))
            y_s, s_s = _retention(xs_pad.reshape(bs * RET_CHUNK, D_MODEL), w_in, cos_s, sin_s, gg, gb,
                                  state_ret[o], tm=RET_CHUNK, seq_len=RET_CHUNK, decay_len=ts,
                                  name="retention_sample")
            y_s = y_s.reshape(bs, RET_CHUNK, H_C * DV_C)[:, :ts].reshape(bs * ts, H_C * DV_C)
            w_out = w_out_ret[o].astype(BF16)
            g1, b1 = row2d(ln1_g[l]), row2d(ln1_b[l])
            mix_p = (xp, y_p, y_p, 1, w_out, g1, b1)
            mix_s = (xs, y_s, y_s, 1, w_out, g1, b1)
            ret_p.append(s_p)
            ret_s.append(s_s)

        wup = w_up[l].astype(BF16)
        wdn = w_down[l].astype(BF16)
        g2, b2 = row2d(ln2_g[l]), row2d(ln2_b[l])
        xp, cp = _ffn_prompt(mix_p, wup, conv_w[l], row2d(conv_b[l]), wdn, g2, b2, tm=512, seq_len=tp)
        xs, cs = _ffn_sample(mix_s, wup, conv_w[l], row2d(conv_b[l]), wdn, g2, b2, state_ffn_conv[l], seq_len=ts)
        conv_p.append(cp)
        conv_s.append(cs)

    return (xp.reshape(bp, tp, D_MODEL), xs.reshape(bs, ts, D_MODEL),
            jnp.stack(sb_k_p), jnp.stack(sb_v_p), jnp.stack(sb_k_s), jnp.stack(sb_v_s),
            jnp.stack(chunk_v_s), jnp.stack(ret_p), jnp.stack(ret_s), jnp.stack(conv_p), jnp.stack(conv_s))
```

```python
import functools
import math

import jax
import jax.numpy as jnp
from jax import lax
from jax.experimental import pallas as pl
from jax.experimental.pallas import tpu as pltpu

F32 = jnp.float32
BF16 = jnp.bfloat16

D_MODEL = 1024
W_A = D_MODEL // 2
G_A = 8
DG_A = W_A // G_A
CHUNK = 128
W_B = D_MODEL // 2
H_B = 8
DH_B = W_B // H_B
PAGE_SIZE = 128
H_C = 4
DK_C = D_MODEL // H_C
DV_C = 2 * DK_C
RET_CHUNK = 128
ROPE_BASE = 10000.0
D_FF = ((8 * D_MODEL // 3 + 127) // 128) * 128
CONV_W = 3
DEPTH = 2
ALPHA = (2 * DEPTH) ** 0.25
LN_EPS = 1e-5
LOG2E = 1.4426950408889634
EXP2_CLAMP = 126.0
BIAS_PIECES = 3
DECODE_GROUP = 16

LANES = 128
FF_CHUNK = 256
VMEM_LIMIT = 56 * 1024 * 1024

_LOG_G = tuple(math.log1p(-(2.0 ** (-5.0 - h))) for h in range(H_C))


def _cparams(n_axes):
    return pltpu.CompilerParams(
        dimension_semantics=("arbitrary",) * n_axes, vmem_limit_bytes=VMEM_LIMIT)


def _const_spec(shape):
    nd = len(shape)
    return pl.BlockSpec(shape, lambda *_: (0,) * nd, pipeline_mode=pl.Buffered(1))


def _ln(x, g, b):
    mu = jnp.mean(x, axis=-1, keepdims=True)
    xc = x - mu
    var = jnp.mean(xc * xc, axis=-1, keepdims=True)
    return xc * lax.rsqrt(var + LN_EPS) * g + b


def _softplus2(w):
    return jnp.maximum(w, jnp.log2(1.0 + jnp.exp2(jnp.minimum(w, EXP2_CLAMP))))


def _split_bf16(x):
    hi = x.astype(BF16)
    lo = (x - hi.astype(F32)).astype(BF16)
    return jnp.concatenate([hi, lo], axis=1)


def _emit_skewed(chains):
    depth = max(len(chain) for chain in chains)
    for t in range(len(chains) + depth - 1):
        for s in range(depth):
            if 0 <= t - s < len(chains) and s < len(chains[t - s]):
                chains[t - s][s]()


def _neg_suffix_matrix(n, parts, inclusive):
    j = lax.broadcasted_iota(jnp.int32, (parts * n, n), 0) % n
    s = lax.broadcasted_iota(jnp.int32, (parts * n, n), 1)
    return jnp.where((j >= s) if inclusive else (j > s), -1.0, 0.0).astype(BF16)


def _ab_proj_kernel(x_ref, w_ref, vg_ref, vb_ref, wmix_ref, mixb_ref, *refs, mix_len, head_major):
    if head_major:
        a_ref, q_ref, k_ref, v_ref, kf_ref, vf_ref, mixed_ref = refs
    else:
        a_ref, q_ref, kf_ref, vf_ref, va_ref, mixed_ref = refs
    tm = x_ref.shape[0]
    x = x_ref[...].astype(BF16)
    h = jnp.dot(x, w_ref[...], preferred_element_type=F32)
    u = jax.nn.gelu(h[:, :W_A])
    va = _ln(jax.nn.gelu(h[:, W_A:2 * W_A]), vg_ref[...], vb_ref[...])
    q = h[:, 2 * W_A:2 * W_A + W_B] * (DH_B ** -0.5 * LOG2E)
    k = h[:, 2 * W_A + W_B:2 * W_A + 2 * W_B]
    v = h[:, 2 * W_A + 2 * W_B:]
    if head_major:
        for p in range(W_B // LANES):
            sl = slice(p * LANES, (p + 1) * LANES)
            q_ref[0, p] = q[:, sl].astype(BF16)
            k_ref[0, p] = k[:, sl].astype(BF16)
            v_ref[0, p] = v[:, sl].astype(BF16)
        for pg in range(tm // PAGE_SIZE):
            rows = slice(pg * PAGE_SIZE, (pg + 1) * PAGE_SIZE)
            kf_ref[pg] = k[rows].T
            vf_ref[pg] = v[rows].T
    else:
        kf_ref[...] = k
        vf_ref[...] = v
        q_ref[...] = q
        va_ref[...] = va

    r = lax.broadcasted_iota(jnp.int32, (CHUNK, CHUNK), 0)
    c = lax.broadcasted_iota(jnp.int32, (CHUNK, CHUNK), 1)
    if mix_len == CHUNK:
        mask = c <= r
    else:
        mask = jnp.logical_and(r // mix_len == c // mix_len, c % mix_len <= r % mix_len)
    wg = [jnp.where(mask, wmix_ref[g], 0.0).astype(BF16) for g in range(G_A)]
    for ci in range(tm // CHUNK):
        rows = slice(ci * CHUNK, (ci + 1) * CHUNK)
        va_c = va[rows]
        for g in range(G_A):
            cols = slice(g * DG_A, (g + 1) * DG_A)
            mixed_ref[rows, cols] = jnp.dot(wg[g], va_c[:, cols].astype(BF16),
                                            preferred_element_type=F32)
        a_ref[rows, :] = (u[rows] * (mixed_ref[rows, :] + mixb_ref[...])).astype(BF16)


def _ab_proj(x2d, w_bf, vg, vb, wmix, mixb, *, mix_len, head_major, tm, seq_len=None):
    rows = x2d.shape[0]
    d_in = w_bf.shape[1]
    n_tiles = rows // tm
    row_spec = lambda width: pl.BlockSpec((tm, width), lambda i: (i, 0))
    in_specs = [row_spec(D_MODEL), _const_spec((D_MODEL, d_in)), _const_spec((1, W_A)),
                _const_spec((1, W_A)), _const_spec((G_A, CHUNK, CHUNK)), _const_spec((CHUNK, W_A))]
    if head_major:
        tps = seq_len // tm
        batch = rows // seq_len
        hm_spec = pl.BlockSpec((1, W_B // LANES, tm, LANES), lambda i: (i // tps, 0, i % tps, 0))
        hm_shape = jax.ShapeDtypeStruct((batch, W_B // LANES, seq_len, LANES), BF16)
        ppt = tm // PAGE_SIZE
        pg_spec = pl.BlockSpec((ppt, W_B, PAGE_SIZE), lambda i: (i, 0, 0))
        pg_shape = jax.ShapeDtypeStruct((rows // PAGE_SIZE, W_B, PAGE_SIZE), F32)
        out_specs = [row_spec(W_A), hm_spec, hm_spec, hm_spec, pg_spec, pg_spec]
        out_shape = [jax.ShapeDtypeStruct((rows, W_A), BF16), hm_shape, hm_shape, hm_shape,
                     pg_shape, pg_shape]
    else:
        out_specs = [row_spec(W_A), row_spec(W_B), row_spec(W_B), row_spec(W_B), row_spec(W_A)]
        out_shape = [jax.ShapeDtypeStruct((rows, W_A), BF16)] + [
            jax.ShapeDtypeStruct((rows, W_B), F32)] * 3 + [jax.ShapeDtypeStruct((rows, W_A), F32)]
    return pl.pallas_call(
        functools.partial(_ab_proj_kernel, mix_len=mix_len, head_major=head_major),
        grid=(n_tiles,), in_specs=in_specs, out_specs=out_specs, out_shape=out_shape,
        scratch_shapes=[pltpu.VMEM((tm, W_A), F32)],
        compiler_params=_cparams(1),
        name="ab_proj_prompt" if head_major else "ab_proj_sample",
    )(x2d, w_bf, vg, vb, wmix, mixb)


def _attn_step(qi, bias_ref, q_ref, k_ref, v_ref, o_ref, acc_ref, carry_ref, blk, extra_chains):
    n_pairs = q_ref.shape[1]
    n_heads = 2 * n_pairs
    lane = lax.broadcasted_iota(jnp.int32, (blk, LANES), 1)
    neg_u = _neg_suffix_matrix(blk, 1, inclusive=False)
    row = lax.broadcasted_iota(jnp.int32, (blk, blk), 0)
    col = lax.broadcasted_iota(jnp.int32, (blk, blk), 1)
    causal = col < row
    k_ones = jnp.where(lane < BIAS_PIECES, 1.0, 0.0).astype(BF16)
    q_aug = []
    for h in range(n_heads):
        q2 = q_ref[0, h // 2]
        qh = jnp.where((lane >= DH_B) == bool(h % 2), q2, jnp.zeros_like(q2))
        q_aug.append(jnp.concatenate(
            [qh, jnp.broadcast_to(bias_ref[h:h + 1, :], (blk, LANES))], axis=1))

    def start_of(kb):
        return kb * blk if isinstance(kb, int) else pl.multiple_of(kb * blk, blk)

    def block_chains(kb, diagonal):
        start = start_of(kb)
        vals = [dict() for _ in range(n_heads)]

        def logits(h):
            k2 = k_ref[0, h // 2, pl.ds(start, blk), :]
            vals[h]["w"] = lax.dot_general(q_aug[h], jnp.concatenate([k2, k_ones], axis=1),
                                           (((1,), (1,)), ((), ())), preferred_element_type=F32)

        def keep(h):
            w = vals[h].pop("w")
            sp = _softplus2(w)
            if diagonal:
                sp = jnp.where(causal, sp, 0.0)
                vals[h]["d"] = w - sp
                carry_ref[h] = jnp.broadcast_to(-jnp.sum(sp, axis=1, keepdims=True), (blk, LANES))
            else:
                carry = carry_ref[h]
                vals[h]["d"] = w - sp + jnp.concatenate([carry] * (blk // LANES), axis=1)
                carry_ref[h] = carry - jnp.sum(sp, axis=1, keepdims=True)
            vals[h]["sp"] = sp.astype(BF16)

        def suffix(h):
            vals[h]["suf"] = jnp.dot(vals[h].pop("sp"), neg_u, preferred_element_type=F32)

        def weights(h):
            a = jnp.exp2(vals[h].pop("d") + vals[h].pop("suf"))
            if diagonal:
                a = jnp.where(causal, a, 0.0)
            vals[h]["a"] = a.astype(BF16)

        def values(h):
            v2 = v_ref[0, h // 2, pl.ds(start, blk), :]
            pv = jnp.dot(vals[h].pop("a"), v2, preferred_element_type=F32)
            if diagonal:
                acc_ref[h] = pv
            else:
                acc_ref[h] += pv

        return [[functools.partial(stage, h) for stage in (logits, keep, suffix, weights, values)]
                for h in range(n_heads)]

    def blocks(kbs, diagonal):
        _emit_skewed([chain for kb in kbs for chain in block_chains(kb, diagonal)])

    diag = block_chains(qi, True)
    every = max(1, len(diag) // max(1, len(extra_chains)))
    merged = []
    for i, chain in enumerate(diag):
        if i % every == 0 and i // every < len(extra_chains):
            merged.append(extra_chains[i // every])
        merged.append(chain)
    _emit_skewed(merged)

    def pair(j, _):
        blocks([qi - 1 - 2 * j, qi - 2 - 2 * j], False)
        return 0

    lax.fori_loop(0, qi // 2, pair, 0)

    @pl.when(qi % 2 == 1)
    def _():
        blocks([0], False)

    for hp in range(n_pairs):
        o_ref[0, :, hp * LANES:(hp + 1) * LANES] = jnp.where(
            lane < DH_B, acc_ref[2 * hp], acc_ref[2 * hp + 1]).astype(o_ref.dtype)


def _bias_rows(bias2):
    pieces, rest = [], bias2.astype(F32)
    for _ in range(BIAS_PIECES):
        piece = rest.astype(BF16)
        pieces.append(piece)
        rest = rest - piece.astype(F32)
    rows = jnp.stack(pieces, axis=1)
    return jnp.pad(rows, ((0, 0), (0, LANES - BIAS_PIECES)))


def _decode_chains(qbd_ref, bias_ref, k_pgs, v_pgs, acc_ref, carry_ref, valid=None):
    n_rows = qbd_ref.shape[1]
    neg_u = _neg_suffix_matrix(PAGE_SIZE, 2, inclusive=True)

    def chain(k_grp, v_grp):
        n = len(k_grp)
        vals = {}

        def logits():
            k_cat = jnp.concatenate([k[0].astype(BF16) for k in k_grp], axis=1)
            vals["w_cat"] = jnp.dot(qbd_ref[0], k_cat, preferred_element_type=F32)

        def keep():
            w_cat = vals.pop("w_cat")
            w = jnp.concatenate([w_cat[:, p * PAGE_SIZE:(p + 1) * PAGE_SIZE] for p in range(n)], axis=0)
            w = w + jnp.concatenate([bias_ref[...]] * n, axis=0)
            sp = _softplus2(w)
            if valid is not None:
                sp = jnp.where(valid, sp, 0.0)
            tot = jnp.sum(sp, axis=1, keepdims=True)
            carry = carry_ref[...]
            carries = []
            for p in range(n):
                carries.append(carry)
                carry = carry - tot[p * n_rows:(p + 1) * n_rows]
            carry_ref[...] = carry
            vals["d"] = w + jnp.concatenate(carries, axis=0)
            vals["split"] = _split_bf16(sp)

        def suffix():
            vals["suf"] = jnp.dot(vals.pop("split"), neg_u, preferred_element_type=F32)

        def weights():
            a = jnp.exp2(vals.pop("d") + vals.pop("suf"))
            if valid is not None:
                a = jnp.where(valid, a, 0.0)
            a = a.astype(BF16)
            vals["a_cat"] = jnp.concatenate([a[p * n_rows:(p + 1) * n_rows] for p in range(n)], axis=1)

        def values():
            v_cat = jnp.concatenate([v[0].astype(BF16) for v in v_grp], axis=1)
            acc_ref[...] += lax.dot_general(vals.pop("a_cat"), v_cat, (((1,), (1,)), ((), ())),
                                            preferred_element_type=F32)

        return [logits, keep, suffix, weights, values]

    return [chain(k_pgs[first:first + DECODE_GROUP], v_pgs[first:first + DECODE_GROUP])
            for first in range(0, len(k_pgs), DECODE_GROUP)]


def _decode_begin(j, qbd_ref, bias_ref, knew_ref, vnew_ref, acc_ref, carry_ref):
    n_rows = qbd_ref.shape[1]

    @pl.when(j == 0)
    def _():
        acc_ref[...] = jnp.zeros_like(acc_ref)
        carry_ref[...] = jnp.zeros_like(carry_ref)
        t = lax.broadcasted_iota(jnp.int32, (n_rows, PAGE_SIZE), 0) // H_B
        i = lax.broadcasted_iota(jnp.int32, (n_rows, PAGE_SIZE), 1)
        _emit_skewed(_decode_chains(qbd_ref, bias_ref, [knew_ref], [vnew_ref], acc_ref, carry_ref,
                                    valid=i < t))


def _decode_end(j, n_steps, o_ref, acc_ref, n_q):
    n_rows = acc_ref.shape[0]

    @pl.when(j == n_steps - 1)
    def _():
        rr = lax.broadcasted_iota(jnp.int32, (n_rows, W_B), 0) % H_B
        cc = lax.broadcasted_iota(jnp.int32, (n_rows, W_B), 1) // DH_B
        own = jnp.where(rr == cc, acc_ref[...], 0.0)
        o_ref[0] = jnp.sum(own.reshape(n_q, H_B, W_B), axis=1)


def _sb_kernel(pt_ref, bias_ref, q_ref, k_ref, v_ref, qbd_ref, dbias_ref, knew_ref, vnew_ref,
               *refs, blk, pages_per_step, steps_per_seq, n_q):
    del pt_ref
    k_refs = refs[:pages_per_step]
    v_refs = refs[pages_per_step:2 * pages_per_step]
    o_ref, od_ref, acc_ref, carry_ref, dacc_ref, dcarry_ref = refs[2 * pages_per_step:]
    j = (pl.program_id(0) * pl.num_programs(1) + pl.program_id(1)) % steps_per_seq
    _decode_begin(j, qbd_ref, dbias_ref, knew_ref, vnew_ref, dacc_ref, dcarry_ref)
    decode = _decode_chains(qbd_ref, dbias_ref, k_refs, v_refs, dacc_ref, dcarry_ref)
    _attn_step(pl.program_id(1), bias_ref, q_ref, k_ref, v_ref, o_ref, acc_ref, carry_ref, blk, decode)
    _decode_end(j, steps_per_seq, od_ref, dacc_ref, n_q)


def _sb_attention(q_hm, k_hm, v_hm, bias2, page_table, qbd, bias_tile, k_new, v_new, pool_k, pool_v,
                  *, blk):
    batch, n_pairs, seq_len, _ = q_hm.shape
    nq = seq_len // blk
    n_heads = 2 * n_pairs
    n_seq, n_pages = page_table.shape
    n_rows = qbd.shape[1]
    n_q = n_rows // H_B
    n_steps = batch * nq
    pages_per_step = n_seq * n_pages // n_steps
    steps_per_seq = n_pages // pages_per_step
    assert pages_per_step * n_steps == n_seq * n_pages and steps_per_seq * pages_per_step == n_pages
    assert pages_per_step % DECODE_GROUP == 0

    seq_of = lambda b, i: (b * nq + i) // steps_per_seq

    def page_spec(p):
        def index(b, i, pt):
            j = (b * nq + i) % steps_per_seq
            return (pt[seq_of(b, i), n_pages - 1 - (j * pages_per_step + p)], 0, 0)
        return pl.BlockSpec((1, W_B, PAGE_SIZE), index)

    per_seq = lambda shape: pl.BlockSpec((1,) + shape, lambda b, i, pt: (seq_of(b, i), 0, 0))
    kv_spec = pl.BlockSpec((1, n_pairs, seq_len, LANES), lambda b, i, pt: (b, 0, 0, 0),
                           pipeline_mode=pl.Buffered(1))
    grid_spec = pltpu.PrefetchScalarGridSpec(
        num_scalar_prefetch=1, grid=(batch, nq),
        in_specs=[pl.BlockSpec((n_heads, LANES), lambda b, i, pt: (0, 0)),
                  pl.BlockSpec((1, n_pairs, blk, LANES), lambda b, i, pt: (b, 0, i, 0)),
                  kv_spec, kv_spec,
                  per_seq((n_rows, W_B)),
                  pl.BlockSpec((n_rows, LANES), lambda b, i, pt: (0, 0)),
                  per_seq((W_B, PAGE_SIZE)), per_seq((W_B, PAGE_SIZE))]
                 + [page_spec(p) for p in range(pages_per_step)] * 2,
        out_specs=[pl.BlockSpec((1, blk, n_pairs * LANES), lambda b, i, pt: (b, i, 0)),
                   per_seq((n_q, W_B))],
        scratch_shapes=[pltpu.VMEM((n_heads, blk, LANES), F32), pltpu.VMEM((n_heads, blk, LANES), F32),
                        pltpu.VMEM((n_rows, W_B), F32), pltpu.VMEM((n_rows, LANES), F32)])
    return pl.pallas_call(
        functools.partial(_sb_kernel, blk=blk, pages_per_step=pages_per_step,
                          steps_per_seq=steps_per_seq, n_q=n_q),
        grid_spec=grid_spec,
        out_shape=[jax.ShapeDtypeStruct((batch, seq_len, n_pairs * LANES), BF16),
                   jax.ShapeDtypeStruct((n_seq, n_q, W_B), F32)],
        compiler_params=_cparams(2), name="sb_attention",
    )(page_table, _bias_rows(bias2), q_hm, k_hm, v_hm, qbd, bias_tile, k_new, v_new,
      *([pool_k] * pages_per_step), *([pool_v] * pages_per_step))


def _mixer_out_ln(x_ref, a1_ref, a2_ref, wo_ref, g_ref, b_ref):
    a = jnp.concatenate([a1_ref[...], a2_ref[...]], axis=1)
    y = jnp.dot(a, wo_ref[...], preferred_element_type=F32)
    return _ln(ALPHA * x_ref[...] + y, g_ref[...], b_ref[...])


def _mixer_specs(tm, row_map, k1, a2_block_col, w_shape):
    i0 = lambda *idx: (row_map(*idx), 0)
    i2 = lambda *idx: (row_map(*idx), a2_block_col)
    return [pl.BlockSpec((tm, D_MODEL), i0), pl.BlockSpec((tm, k1), i0), pl.BlockSpec((tm, k1), i2),
            _const_spec(w_shape), _const_spec((1, D_MODEL)), _const_spec((1, D_MODEL))]


def _ffn_chunk(x_bf, wup_ref, cw_ref, cb_ref, ci, shift_fix):
    cols = slice(ci * FF_CHUNK, (ci + 1) * FF_CHUNK)
    vcols = slice(D_FF + ci * FF_CHUNK, D_FF + (ci + 1) * FF_CHUNK)
    g = jnp.dot(x_bf, wup_ref[:, cols], preferred_element_type=F32)
    val = jnp.dot(x_bf, wup_ref[:, vcols], preferred_element_type=F32)
    g1, g2 = shift_fix(g, cols)
    cw = cw_ref[:, cols]
    conv = cb_ref[:, cols] + cw[0:1] * g2 + cw[1:2] * g1 + cw[2:3] * g
    return g, (jax.nn.gelu(conv) * val).astype(BF16)


def _ffn_prompt_kernel(x_ref, a1_ref, a2_ref, wo_ref, g1_ref, b1_ref,
                       wup_ref, cw_ref, cb_ref, wdn_ref, g_ref, b_ref,
                       y_ref, conv_ref, act_ref, prev_ref):
    tm = x_ref.shape[0]

    @pl.when(pl.program_id(1) == 0)
    def _():
        prev_ref[...] = jnp.zeros_like(prev_ref)

    x = _mixer_out_ln(x_ref, a1_ref, a2_ref, wo_ref, g1_ref, b1_ref)
    x_bf = x.astype(BF16)
    row = lax.broadcasted_iota(jnp.int32, (tm, FF_CHUNK), 0)

    def shift_fix(g, cols):
        p0 = prev_ref[0:1, cols]
        p1 = prev_ref[1:2, cols]
        g1 = jnp.where(row == 0, p1, pltpu.roll(g, 1, 0))
        g2 = jnp.where(row == 0, p0, jnp.where(row == 1, p1, pltpu.roll(g, 2, 0)))
        return g1, g2

    for ci in range(D_FF // FF_CHUNK):
        cols = slice(ci * FF_CHUNK, (ci + 1) * FF_CHUNK)
        g, act = _ffn_chunk(x_bf, wup_ref, cw_ref, cb_ref, ci, shift_fix)
        act_ref[:, cols] = act
        prev_ref[0:2, cols] = g[tm - 2:tm]
        conv_ref[0, :, cols] = g[tm - 2:tm]
    y = jnp.dot(act_ref[...], wdn_ref[...], preferred_element_type=F32)
    y_ref[...] = _ln(ALPHA * x + y, g_ref[...], b_ref[...])


def _ffn_sample_kernel(x_ref, a1_ref, a2_ref, wo_ref, g1_ref, b1_ref,
                       wup_ref, cw_ref, cb_ref, wdn_ref, g_ref, b_ref, inj1_ref, inj2_ref,
                       y_ref, gate_ref, act_ref, *, seq_len):
    tm = x_ref.shape[0]
    x = _mixer_out_ln(x_ref, a1_ref, a2_ref, wo_ref, g1_ref, b1_ref)
    x_bf = x.astype(BF16)
    pos = lax.broadcasted_iota(jnp.int32, (tm, FF_CHUNK), 0) % seq_len

    def shift_fix(g, cols):
        g1 = jnp.where(pos >= 1, pltpu.roll(g, 1, 0), inj1_ref[:, cols])
        g2 = jnp.where(pos >= 2, pltpu.roll(g, 2, 0), inj2_ref[:, cols])
        return g1, g2

    for ci in range(D_FF // FF_CHUNK):
        cols = slice(ci * FF_CHUNK, (ci + 1) * FF_CHUNK)
        g, act = _ffn_chunk(x_bf, wup_ref, cw_ref, cb_ref, ci, shift_fix)
        act_ref[:, cols] = act
        gate_ref[:, cols] = g
    y = jnp.dot(act_ref[...], wdn_ref[...], preferred_element_type=F32)
    y_ref[...] = _ln(ALPHA * x + y, g_ref[...], b_ref[...])


def _ffn_weight_specs():
    return [_const_spec((D_MODEL, 2 * D_FF)), _const_spec((CONV_W, D_FF)), _const_spec((1, D_FF)),
            _const_spec((D_FF, D_MODEL)), _const_spec((1, D_MODEL)), _const_spec((1, D_MODEL))]


def _ffn_prompt(mixer, wup_bf, cw, cb, wdn_bf, g, b, *, tm, seq_len):
    x2d, a1, a2, a2_col, wo, g1, b1 = mixer
    rows = x2d.shape[0]
    batch = rows // seq_len
    tps = seq_len // tm
    row_map = lambda bi, t: bi * tps + t
    return pl.pallas_call(
        _ffn_prompt_kernel, grid=(batch, tps),
        in_specs=_mixer_specs(tm, row_map, wo.shape[0] // 2, a2_col, wo.shape) + _ffn_weight_specs(),
        out_specs=[pl.BlockSpec((tm, D_MODEL), lambda bi, t: (bi * tps + t, 0)),
                   pl.BlockSpec((1, CONV_W - 1, D_FF), lambda bi, t: (bi, 0, 0))],
        out_shape=[jax.ShapeDtypeStruct((rows, D_MODEL), F32),
                   jax.ShapeDtypeStruct((batch, CONV_W - 1, D_FF), F32)],
        scratch_shapes=[pltpu.VMEM((tm, D_FF), BF16), pltpu.VMEM((8, D_FF), F32)],
        compiler_params=_cparams(2), name="ffn_prompt",
    )(x2d, a1, a2, wo, g1, b1, wup_bf, cw, cb, wdn_bf, g, b)


def _ffn_sample(mixer, wup_bf, cw, cb, wdn_bf, g, b, conv_state, *, seq_len):
    x2d, a1, a2, a2_col, wo, g1, b1 = mixer
    rows = x2d.shape[0]
    n_seq = rows // seq_len
    zeros = jnp.zeros((n_seq, 1, D_FF), F32)
    inj1 = jnp.concatenate([conv_state[:, 1:2]] + [zeros] * (seq_len - 1), axis=1).reshape(rows, D_FF)
    inj2 = jnp.concatenate([conv_state] + [zeros] * (seq_len - 2), axis=1).reshape(rows, D_FF)
    full = lambda width: pl.BlockSpec((rows, width), lambda i: (0, 0))
    y, gate = pl.pallas_call(
        functools.partial(_ffn_sample_kernel, seq_len=seq_len), grid=(1,),
        in_specs=(_mixer_specs(rows, lambda i: 0, wo.shape[0] // 2, a2_col, wo.shape)
                  + _ffn_weight_specs() + [full(D_FF), full(D_FF)]),
        out_specs=[full(D_MODEL), full(D_FF)],
        out_shape=[jax.ShapeDtypeStruct((rows, D_MODEL), F32), jax.ShapeDtypeStruct((rows, D_FF), F32)],
        scratch_shapes=[pltpu.VMEM((rows, D_FF), BF16)],
        compiler_params=_cparams(1), name="ffn_sample",
    )(x2d, a1, a2, wo, g1, b1, wup_bf, cw, cb, wdn_bf, g, b, inj1, inj2)
    new_state = gate.reshape(n_seq, seq_len, D_FF)[:, seq_len - (CONV_W - 1):]
    return y, new_state


def _ret_kernel(x_ref, w_ref, cos_ref, sin_ref, gg_ref, gb_ref, *refs, decay_len, has_state):
    if has_state:
        s0_ref, y_ref, sout_ref, s_ref = refs
    else:
        y_ref, sout_ref, s_ref = refs
    tm = x_ref.shape[0]
    t = pl.program_id(1)

    @pl.when(t == 0)
    def _():
        if has_state:
            s_ref[...] = s0_ref[0]
        else:
            s_ref[...] = jnp.zeros_like(s_ref)

    x = x_ref[...].astype(BF16)
    cos = cos_ref[...]
    sin = sin_ref[...]
    half = DK_C // 2

    def rotary(a):
        a1, a2 = a[:, :half], a[:, half:]
        return jnp.concatenate([a1 * cos - a2 * sin, a1 * sin + a2 * cos], axis=1)

    ii = lax.broadcasted_iota(jnp.int32, (RET_CHUNK, RET_CHUNK), 0)
    jj = lax.broadcasted_iota(jnp.int32, (RET_CHUNK, RET_CHUNK), 1)
    diff = jnp.where(ii >= jj, ii - jj, 0).astype(F32)
    idx = lax.broadcasted_iota(jnp.int32, (RET_CHUNK, 1), 0).astype(F32)
    idx_tile = (lax.broadcasted_iota(jnp.int32, (tm, 1), 0) % RET_CHUNK).astype(F32)
    v_off = 2 * H_C * DK_C
    g_off = v_off + H_C * DV_C
    n_chunks = tm // RET_CHUNK
    heads = [dict() for _ in range(H_C)]

    def project(h):
        q = rotary(jnp.dot(x, w_ref[:, h * DK_C:(h + 1) * DK_C], preferred_element_type=F32))
        k = rotary(jnp.dot(x, w_ref[:, (H_C + h) * DK_C:(H_C + h + 1) * DK_C],
                           preferred_element_type=F32)) * (DK_C ** -0.5)
        v = jnp.dot(x, w_ref[:, v_off + h * DV_C:v_off + (h + 1) * DV_C], preferred_element_type=F32)
        k_dec = jnp.exp((decay_len - 1.0 - idx_tile) * _LOG_G[h])
        heads[h].update(q=q.astype(BF16), k=k.astype(BF16), v=v.astype(BF16),
                        kd_t=(k * k_dec).T.astype(BF16))

    def project_gate(h):
        heads[h]["gate"] = jnp.dot(x, w_ref[:, g_off + h * DV_C:g_off + (h + 1) * DV_C],
                                   preferred_element_type=F32)

    chunks = [slice(ci * RET_CHUNK, (ci + 1) * RET_CHUNK) for ci in range(n_chunks)]

    def advance_state(h):
        c_dec = math.exp(decay_len * _LOG_G[h])
        v, kd_t = heads[h]["v"], heads[h].pop("kd_t")
        kvs = [jnp.dot(kd_t[:, rows], v[rows], preferred_element_type=F32) for rows in chunks]
        states = [s_ref[h]]
        for kv in kvs:
            states.append(states[-1] * c_dec + kv)
        s_ref[h] = states[-1]
        heads[h]["states"] = [state.astype(BF16) for state in states[:-1]]

    def retain(h):
        log_g = _LOG_G[h]
        dmat = jnp.where(ii >= jj, jnp.exp(diff * log_g), 0.0)
        q_dec = jnp.exp((idx + 1.0) * log_g)
        q, k, v, states = (heads[h].pop(name) for name in ("q", "k", "v", "states"))
        outs = [None] * n_chunks
        vals = [dict() for _ in chunks]

        def scores(ci):
            vals[ci]["sc"] = lax.dot_general(q[chunks[ci]], k[chunks[ci]], (((1,), (1,)), ((), ())),
                                             preferred_element_type=F32)

        def decay(ci):
            vals[ci]["sc"] = (vals[ci]["sc"] * dmat).astype(BF16)

        def output(ci):
            rows = chunks[ci]
            outs[ci] = (jnp.dot(vals[ci].pop("sc"), v[rows], preferred_element_type=F32)
                        + jnp.dot(q[rows], states[ci], preferred_element_type=F32) * q_dec)

        _emit_skewed([[functools.partial(stage, ci) for stage in (scores, decay, output)]
                      for ci in range(n_chunks)])
        heads[h]["o"] = outs

    def finish(h):
        vcols = slice(h * DV_C, (h + 1) * DV_C)
        gate = heads[h].pop("gate")
        for ci, o in enumerate(heads[h].pop("o")):
            rows = slice(ci * RET_CHUNK, (ci + 1) * RET_CHUNK)
            mu = jnp.mean(o, axis=-1, keepdims=True)
            oc = o - mu
            var = jnp.mean(oc * oc, axis=-1, keepdims=True)
            normed = oc * lax.rsqrt(var + LN_EPS) * gg_ref[:, vcols] + gb_ref[:, vcols]
            y_ref[rows, vcols] = (jax.nn.silu(gate[rows]) * normed).astype(y_ref.dtype)

    project(0)
    for h in range(H_C):
        advance_state(h)
        if h + 1 < H_C:
            project(h + 1)
        retain(h)
        project_gate(h)
        if h > 0:
            finish(h - 1)
    finish(H_C - 1)

    @pl.when(t == pl.num_programs(1) - 1)
    def _():
        sout_ref[0] = s_ref[...]


def _retention(x2d, w_bf, cos, sin, gg, gb, state0, *, tm, seq_len, decay_len, name):
    rows = x2d.shape[0]
    n_seq = rows // seq_len
    tps = seq_len // tm
    d_y = H_C * DV_C
    has_state = state0 is not None
    state_spec = pl.BlockSpec((1, H_C, DK_C, DV_C), lambda bi, t: (bi, 0, 0, 0))
    in_specs = [pl.BlockSpec((tm, D_MODEL), lambda bi, t: (bi * tps + t, 0)),
                _const_spec(w_bf.shape),
                pl.BlockSpec((tm, DK_C // 2), lambda bi, t: (t, 0)),
                pl.BlockSpec((tm, DK_C // 2), lambda bi, t: (t, 0)),
                _const_spec((1, d_y)), _const_spec((1, d_y))]
    args = [x2d, w_bf, cos, sin, gg, gb]
    if has_state:
        in_specs.append(state_spec)
        args.append(state0)
    return pl.pallas_call(
        functools.partial(_ret_kernel, decay_len=decay_len, has_state=has_state),
        grid=(n_seq, tps), in_specs=in_specs,
        out_specs=[pl.BlockSpec((tm, d_y), lambda bi, t: (bi * tps + t, 0)), state_spec],
        out_shape=[jax.ShapeDtypeStruct((rows, d_y), BF16),
                   jax.ShapeDtypeStruct((n_seq, H_C, DK_C, DV_C), F32)],
        scratch_shapes=[pltpu.VMEM((H_C, DK_C, DV_C), F32)],
        compiler_params=_cparams(2), name=name,
    )(*args)


def _rope_tables(pos):
    half = DK_C // 2
    inv = ROPE_BASE ** (-jnp.arange(half, dtype=F32) / half)
    ang = pos.astype(F32)[:, None] * inv[None, :]
    return jnp.cos(ang), jnp.sin(ang)


def kernel(x_prompt, x_sample, cache_sb_k, cache_sb_v, state_ret, state_ffn_conv, page_table,
           w_in_ab, vln_g, vln_b, w_s, b_s, sb_bias, w_out_ab, w_in_ret, gn_g, gn_b, w_out_ret,
           ln1_g, ln1_b, ln2_g, ln2_b, w_up, conv_w, conv_b, w_down):
    bp, tp, _ = x_prompt.shape
    bs, ts, _ = x_sample.shape
    n_pages = page_table.shape[1]
    past_len = n_pages * PAGE_SIZE
    n_phys = cache_sb_k.shape[1]
    xp = x_prompt.reshape(bp * tp, D_MODEL)
    xs = x_sample.reshape(bs * ts, D_MODEL)
    row2d = lambda a: a.reshape(1, -1)

    sb_k_p, sb_v_p, sb_k_s, sb_v_s, chunk_v_s = [], [], [], [], []
    ret_p, ret_s, conv_p, conv_s = [], [], [], []
    for l in range(DEPTH):
        if l % 2 == 0:
            e = l // 2
            w_in = w_in_ab[e].astype(BF16)
            vg, vb = row2d(vln_g[e]), row2d(vln_b[e])
            mixb_p = jnp.repeat(b_s[e].T, DG_A, axis=1)
            a_p, q_hm, k_hm, v_hm, kf_p, vf_p = _ab_proj(
                xp, w_in, vg, vb, w_s[e], mixb_p, mix_len=CHUNK, head_major=True, tm=512, seq_len=tp)
            bias2 = sb_bias[e] * LOG2E

            reps = CHUNK // ts
            wmix_s = jnp.tile(w_s[e][:, :ts, :ts], (1, reps, reps))
            mixb_s = jnp.repeat(jnp.tile(b_s[e][:, :ts], (1, reps)).T, DG_A, axis=1)
            a_s, q_s, kf_s, vf_s, va_s = _ab_proj(
                xs, w_in, vg, vb, wmix_s, mixb_s, mix_len=ts, head_major=False, tm=bs * ts)
            head_of_col = jnp.arange(W_B) // DH_B
            own = (jnp.arange(H_B)[:, None] == head_of_col[None, :]).astype(F32)
            qbd = (q_s.reshape(bs, ts, 1, W_B) * own[None, None]).reshape(bs, ts * H_B, W_B).astype(BF16)
            bias_tile = jnp.broadcast_to(jnp.tile(bias2, ts)[:, None], (ts * H_B, LANES))
            pad = ((0, 0), (0, 0), (0, PAGE_SIZE - ts))
            k_new = jnp.pad(kf_s.reshape(bs, ts, W_B).transpose(0, 2, 1), pad)
            v_new = jnp.pad(vf_s.reshape(bs, ts, W_B).transpose(0, 2, 1), pad)
            pool_k = cache_sb_k[e].transpose(0, 2, 3, 1).reshape(n_phys, W_B, PAGE_SIZE)
            pool_v = cache_sb_v[e].transpose(0, 2, 3, 1).reshape(n_phys, W_B, PAGE_SIZE)
            b_p, b_s_out = _sb_attention(q_hm, k_hm, v_hm, bias2, page_table, qbd, bias_tile,
                                         k_new, v_new, pool_k, pool_v, blk=256)

            w_out = w_out_ab[e].astype(BF16)
            g1, b1 = row2d(ln1_g[l]), row2d(ln1_b[l])
            mix_p = (xp, a_p, b_p.reshape(bp * tp, W_B), 0, w_out, g1, b1)
            mix_s = (xs, a_s, b_s_out.reshape(bs * ts, W_B).astype(BF16), 0, w_out, g1, b1)
            pages_of = lambda a: a.reshape(bp, tp // PAGE_SIZE, H_B, DH_B, PAGE_SIZE).transpose(0, 1, 4, 2, 3)
            sb_k_p.append(pages_of(kf_p))
            sb_v_p.append(pages_of(vf_p))
            sb_k_s.append(kf_s.reshape(bs, ts, H_B, DH_B))
            sb_v_s.append(vf_s.reshape(bs, ts, H_B, DH_B))
            chunk_v_s.append(va_s.reshape(bs, ts, G_A, DG_A))
        else:
            o = l // 2
            w_in = w_in_ret[o].astype(BF16)
            gg, gb = row2d(gn_g[o]), row2d(gn_b[o])
            cos_p, sin_p = _rope_tables(jnp.arange(tp))
            y_p, s_p = _retention(xp, w_in, cos_p, sin_p, gg, gb, None, tm=512, seq_len=tp,
                                  decay_len=RET_CHUNK, name="retention_prompt")
            cos_s, sin_s = _rope_tables(past_len + jnp.arange(RET_CHUNK))
            xs_pad = jnp.pad(xs.reshape(bs, ts, D_MODEL), ((0, 0), (0, RET_CHUNK - ts), (0, 0)))
            y_s, s_s = _retention(xs_pad.reshape(bs * RET_CHUNK, D_MODEL), w_in, cos_s, sin_s, gg, gb,
                                  state_ret[o], tm=RET_CHUNK, seq_len=RET_CHUNK, decay_len=ts,
                                  name="retention_sample")
            y_s = y_s.reshape(bs, RET_CHUNK, H_C * DV_C)[:, :ts].reshape(bs * ts, H_C * DV_C)
            w_out = w_out_ret[o].astype(BF16)
            g1, b1 = row2d(ln1_g[l]), row2d(ln1_b[l])
            mix_p = (xp, y_p, y_p, 1, w_out, g1, b1)
            mix_s = (xs, y_s, y_s, 1, w_out, g1, b1)
            ret_p.append(s_p)
            ret_s.append(s_s)

        wup = w_up[l].astype(BF16)
        wdn = w_down[l].astype(BF16)
        g2, b2 = row2d(ln2_g[l]), row2d(ln2_b[l])
        xp, cp = _ffn_prompt(mix_p, wup, conv_w[l], row2d(conv_b[l]), wdn, g2, b2, tm=512, seq_len=tp)
        xs, cs = _ffn_sample(mix_s, wup, conv_w[l], row2d(conv_b[l]), wdn, g2, b2, state_ffn_conv[l], seq_len=ts)
        conv_p.append(cp)
        conv_s.append(cs)

    return (xp.reshape(bp, tp, D_MODEL), xs.reshape(bs, ts, D_MODEL),
            jnp.stack(sb_k_p), jnp.stack(sb_v_p), jnp.stack(sb_k_s), jnp.stack(sb_v_s),
            jnp.stack(chunk_v_s), jnp.stack(ret_p), jnp.stack(ret_s), jnp.stack(conv_p), jnp.stack(conv_s))
```

```python
import functools
import math

import jax
import jax.numpy as jnp
from jax import lax
from jax.experimental import pallas as pl
from jax.experimental.pallas import tpu as pltpu

F32 = jnp.float32
BF16 = jnp.bfloat16

D_MODEL = 1024
W_A = D_MODEL // 2
G_A = 8
DG_A = W_A // G_A
CHUNK = 128
W_B = D_MODEL // 2
H_B = 8
DH_B = W_B // H_B
PAGE_SIZE = 128
H_C = 4
DK_C = D_MODEL // H_C
DV_C = 2 * DK_C
RET_CHUNK = 128
ROPE_BASE = 10000.0
D_FF = ((8 * D_MODEL // 3 + 127) // 128) * 128
CONV_W = 3
DEPTH = 2
ALPHA = (2 * DEPTH) ** 0.25
LN_EPS = 1e-5
LOG2E = 1.4426950408889634
EXP2_CLAMP = 126.0
BIAS_PIECES = 3
DECODE_GROUP = 16
LOOP_BLOCKS = 3

LANES = 128
FF_CHUNK = 256
VMEM_LIMIT = 56 * 1024 * 1024

_LOG_G = tuple(math.log1p(-(2.0 ** (-5.0 - h))) for h in range(H_C))


def _cparams(n_axes):
    return pltpu.CompilerParams(
        dimension_semantics=("arbitrary",) * n_axes, vmem_limit_bytes=VMEM_LIMIT)


def _const_spec(shape):
    nd = len(shape)
    return pl.BlockSpec(shape, lambda *_: (0,) * nd, pipeline_mode=pl.Buffered(1))


def _ln(x, g, b):
    mu = jnp.mean(x, axis=-1, keepdims=True)
    xc = x - mu
    var = jnp.mean(xc * xc, axis=-1, keepdims=True)
    return xc * lax.rsqrt(var + LN_EPS) * g + b


def _softplus2(w):
    return jnp.maximum(w, jnp.log2(1.0 + jnp.exp2(jnp.minimum(w, EXP2_CLAMP))))


def _split_bf16(x):
    hi = x.astype(BF16)
    lo = (x - hi.astype(F32)).astype(BF16)
    return jnp.concatenate([hi, lo], axis=1)


def _emit_skewed(chains):
    depth = max(len(chain) for chain in chains)
    for t in range(len(chains) + depth - 1):
        for s in range(depth):
            if 0 <= t - s < len(chains) and s < len(chains[t - s]):
                chains[t - s][s]()


def _neg_suffix_matrix(n, parts, inclusive):
    j = lax.broadcasted_iota(jnp.int32, (parts * n, n), 0) % n
    s = lax.broadcasted_iota(jnp.int32, (parts * n, n), 1)
    return jnp.where((j >= s) if inclusive else (j > s), -1.0, 0.0).astype(BF16)


def _ab_proj_kernel(x_ref, w_ref, vg_ref, vb_ref, wmix_ref, mixb_ref, *refs, mix_len, head_major):
    if head_major:
        a_ref, q_ref, k_ref, v_ref, kf_ref, vf_ref, mixed_ref = refs
    else:
        a_ref, q_ref, kf_ref, vf_ref, va_ref, mixed_ref = refs
    tm = x_ref.shape[0]
    x = x_ref[...].astype(BF16)
    h = jnp.dot(x, w_ref[...], preferred_element_type=F32)
    u = jax.nn.gelu(h[:, :W_A])
    va = _ln(jax.nn.gelu(h[:, W_A:2 * W_A]), vg_ref[...], vb_ref[...])
    q = h[:, 2 * W_A:2 * W_A + W_B] * (DH_B ** -0.5 * LOG2E)
    k = h[:, 2 * W_A + W_B:2 * W_A + 2 * W_B]
    v = h[:, 2 * W_A + 2 * W_B:]
    if head_major:
        for p in range(W_B // LANES):
            sl = slice(p * LANES, (p + 1) * LANES)
            q_ref[0, p] = q[:, sl].astype(BF16)
            k_ref[0, p] = k[:, sl].astype(BF16)
            v_ref[0, p] = v[:, sl].astype(BF16)
        for pg in range(tm // PAGE_SIZE):
            rows = slice(pg * PAGE_SIZE, (pg + 1) * PAGE_SIZE)
            kf_ref[pg] = k[rows].T
            vf_ref[pg] = v[rows].T
    else:
        kf_ref[...] = k
        vf_ref[...] = v
        q_ref[...] = q
        va_ref[...] = va

    r = lax.broadcasted_iota(jnp.int32, (CHUNK, CHUNK), 0)
    c = lax.broadcasted_iota(jnp.int32, (CHUNK, CHUNK), 1)
    if mix_len == CHUNK:
        mask = c <= r
    else:
        mask = jnp.logical_and(r // mix_len == c // mix_len, c % mix_len <= r % mix_len)
    wg = [jnp.where(mask, wmix_ref[g], 0.0).astype(BF16) for g in range(G_A)]
    for ci in range(tm // CHUNK):
        rows = slice(ci * CHUNK, (ci + 1) * CHUNK)
        va_c = va[rows]
        for g in range(G_A):
            cols = slice(g * DG_A, (g + 1) * DG_A)
            mixed_ref[rows, cols] = jnp.dot(wg[g], va_c[:, cols].astype(BF16),
                                            preferred_element_type=F32)
        a_ref[rows, :] = (u[rows] * (mixed_ref[rows, :] + mixb_ref[...])).astype(BF16)


def _ab_proj(x2d, w_bf, vg, vb, wmix, mixb, *, mix_len, head_major, tm, seq_len=None):
    rows = x2d.shape[0]
    d_in = w_bf.shape[1]
    n_tiles = rows // tm
    row_spec = lambda width: pl.BlockSpec((tm, width), lambda i: (i, 0))
    in_specs = [row_spec(D_MODEL), _const_spec((D_MODEL, d_in)), _const_spec((1, W_A)),
                _const_spec((1, W_A)), _const_spec((G_A, CHUNK, CHUNK)), _const_spec((CHUNK, W_A))]
    if head_major:
        tps = seq_len // tm
        batch = rows // seq_len
        hm_spec = pl.BlockSpec((1, W_B // LANES, tm, LANES), lambda i: (i // tps, 0, i % tps, 0))
        hm_shape = jax.ShapeDtypeStruct((batch, W_B // LANES, seq_len, LANES), BF16)
        ppt = tm // PAGE_SIZE
        pg_spec = pl.BlockSpec((ppt, W_B, PAGE_SIZE), lambda i: (i, 0, 0))
        pg_shape = jax.ShapeDtypeStruct((rows // PAGE_SIZE, W_B, PAGE_SIZE), F32)
        out_specs = [row_spec(W_A), hm_spec, hm_spec, hm_spec, pg_spec, pg_spec]
        out_shape = [jax.ShapeDtypeStruct((rows, W_A), BF16), hm_shape, hm_shape, hm_shape,
                     pg_shape, pg_shape]
    else:
        out_specs = [row_spec(W_A), row_spec(W_B), row_spec(W_B), row_spec(W_B), row_spec(W_A)]
        out_shape = [jax.ShapeDtypeStruct((rows, W_A), BF16)] + [
            jax.ShapeDtypeStruct((rows, W_B), F32)] * 3 + [jax.ShapeDtypeStruct((rows, W_A), F32)]
    return pl.pallas_call(
        functools.partial(_ab_proj_kernel, mix_len=mix_len, head_major=head_major),
        grid=(n_tiles,), in_specs=in_specs, out_specs=out_specs, out_shape=out_shape,
        scratch_shapes=[pltpu.VMEM((tm, W_A), F32)],
        compiler_params=_cparams(1),
        name="ab_proj_prompt" if head_major else "ab_proj_sample",
    )(x2d, w_bf, vg, vb, wmix, mixb)


def _attn_step(qi, bias_ref, q_ref, k_ref, v_ref, o_ref, acc_ref, carry_ref, blk, extra_chains):
    n_pairs = q_ref.shape[1]
    n_heads = 2 * n_pairs
    lane = lax.broadcasted_iota(jnp.int32, (blk, LANES), 1)
    neg_u = _neg_suffix_matrix(blk, 1, inclusive=False)
    row = lax.broadcasted_iota(jnp.int32, (blk, blk), 0)
    col = lax.broadcasted_iota(jnp.int32, (blk, blk), 1)
    causal = col < row
    k_ones = jnp.where(lane < BIAS_PIECES, 1.0, 0.0).astype(BF16)
    q_aug = []
    for h in range(n_heads):
        q2 = q_ref[0, h // 2]
        qh = jnp.where((lane >= DH_B) == bool(h % 2), q2, jnp.zeros_like(q2))
        q_aug.append(jnp.concatenate(
            [qh, jnp.broadcast_to(bias_ref[h:h + 1, :], (blk, LANES))], axis=1))

    def start_of(kb):
        return kb * blk if isinstance(kb, int) else pl.multiple_of(kb * blk, blk)

    def block_chains(kb, diagonal):
        start = start_of(kb)
        vals = [dict() for _ in range(n_heads)]

        def logits(h):
            k2 = k_ref[0, h // 2, pl.ds(start, blk), :]
            vals[h]["w"] = lax.dot_general(q_aug[h], jnp.concatenate([k2, k_ones], axis=1),
                                           (((1,), (1,)), ((), ())), preferred_element_type=F32)

        def keep(h):
            w = vals[h].pop("w")
            sp = _softplus2(w)
            if diagonal:
                sp = jnp.where(causal, sp, 0.0)
                vals[h]["d"] = w - sp
                carry_ref[h] = jnp.broadcast_to(-jnp.sum(sp, axis=1, keepdims=True), (blk, LANES))
            else:
                carry = carry_ref[h]
                vals[h]["d"] = w - sp + jnp.concatenate([carry] * (blk // LANES), axis=1)
                carry_ref[h] = carry - jnp.sum(sp, axis=1, keepdims=True)
            vals[h]["sp"] = sp.astype(BF16)

        def suffix(h):
            vals[h]["suf"] = jnp.dot(vals[h].pop("sp"), neg_u, preferred_element_type=F32)

        def weights(h):
            a = jnp.exp2(vals[h].pop("d") + vals[h].pop("suf"))
            if diagonal:
                a = jnp.where(causal, a, 0.0)
            vals[h]["a"] = a.astype(BF16)

        def values(h):
            v2 = v_ref[0, h // 2, pl.ds(start, blk), :]
            pv = jnp.dot(vals[h].pop("a"), v2, preferred_element_type=F32)
            if diagonal:
                acc_ref[h] = pv
            else:
                acc_ref[h] += pv

        return [[functools.partial(stage, h) for stage in (logits, keep, suffix, weights, values)]
                for h in range(n_heads)]

    def blocks(kbs, diagonal):
        _emit_skewed([chain for kb in kbs for chain in block_chains(kb, diagonal)])

    diag = block_chains(qi, True)
    every = max(1, len(diag) // max(1, len(extra_chains)))
    merged = []
    for i, chain in enumerate(diag):
        if i % every == 0 and i // every < len(extra_chains):
            merged.append(extra_chains[i // every])
        merged.append(chain)
    _emit_skewed(merged)

    def trip(j, _):
        blocks([qi - 1 - LOOP_BLOCKS * j - i for i in range(LOOP_BLOCKS)], False)
        return 0

    lax.fori_loop(0, qi // LOOP_BLOCKS, trip, 0)
    for left in range(1, LOOP_BLOCKS):
        pl.when(qi % LOOP_BLOCKS == left)(
            functools.partial(blocks, list(range(left - 1, -1, -1)), False))

    for hp in range(n_pairs):
        o_ref[0, :, hp * LANES:(hp + 1) * LANES] = jnp.where(
            lane < DH_B, acc_ref[2 * hp], acc_ref[2 * hp + 1]).astype(o_ref.dtype)


def _bias_rows(bias2):
    pieces, rest = [], bias2.astype(F32)
    for _ in range(BIAS_PIECES):
        piece = rest.astype(BF16)
        pieces.append(piece)
        rest = rest - piece.astype(F32)
    rows = jnp.stack(pieces, axis=1)
    return jnp.pad(rows, ((0, 0), (0, LANES - BIAS_PIECES)))


def _decode_chains(qbd_ref, bias_ref, k_pgs, v_pgs, acc_ref, carry_ref, valid=None):
    n_rows = qbd_ref.shape[1]
    neg_u = _neg_suffix_matrix(PAGE_SIZE, 2, inclusive=True)

    def chain(k_grp, v_grp):
        n = len(k_grp)
        vals = {}

        def logits():
            k_cat = jnp.concatenate([k[0].astype(BF16) for k in k_grp], axis=1)
            vals["w_cat"] = jnp.dot(qbd_ref[0], k_cat, preferred_element_type=F32)

        def keep():
            w_cat = vals.pop("w_cat")
            w = jnp.concatenate([w_cat[:, p * PAGE_SIZE:(p + 1) * PAGE_SIZE] for p in range(n)], axis=0)
            w = w + jnp.concatenate([bias_ref[...]] * n, axis=0)
            sp = _softplus2(w)
            if valid is not None:
                sp = jnp.where(valid, sp, 0.0)
            tot = jnp.sum(sp, axis=1, keepdims=True)
            carry = carry_ref[...]
            carries = []
            for p in range(n):
                carries.append(carry)
                carry = carry - tot[p * n_rows:(p + 1) * n_rows]
            carry_ref[...] = carry
            vals["d"] = w + jnp.concatenate(carries, axis=0)
            vals["split"] = _split_bf16(sp)

        def suffix():
            vals["suf"] = jnp.dot(vals.pop("split"), neg_u, preferred_element_type=F32)

        def weights():
            a = jnp.exp2(vals.pop("d") + vals.pop("suf"))
            if valid is not None:
                a = jnp.where(valid, a, 0.0)
            a = a.astype(BF16)
            vals["a_cat"] = jnp.concatenate([a[p * n_rows:(p + 1) * n_rows] for p in range(n)], axis=1)

        def values():
            v_cat = jnp.concatenate([v[0].astype(BF16) for v in v_grp], axis=1)
            acc_ref[...] += lax.dot_general(vals.pop("a_cat"), v_cat, (((1,), (1,)), ((), ())),
                                            preferred_element_type=F32)

        return [logits, keep, suffix, weights, values]

    return [chain(k_pgs[first:first + DECODE_GROUP], v_pgs[first:first + DECODE_GROUP])
            for first in range(0, len(k_pgs), DECODE_GROUP)]


def _decode_begin(j, qbd_ref, bias_ref, knew_ref, vnew_ref, acc_ref, carry_ref):
    n_rows = qbd_ref.shape[1]

    @pl.when(j == 0)
    def _():
        acc_ref[...] = jnp.zeros_like(acc_ref)
        carry_ref[...] = jnp.zeros_like(carry_ref)
        t = lax.broadcasted_iota(jnp.int32, (n_rows, PAGE_SIZE), 0) // H_B
        i = lax.broadcasted_iota(jnp.int32, (n_rows, PAGE_SIZE), 1)
        _emit_skewed(_decode_chains(qbd_ref, bias_ref, [knew_ref], [vnew_ref], acc_ref, carry_ref,
                                    valid=i < t))


def _decode_end(j, n_steps, o_ref, acc_ref, n_q):
    n_rows = acc_ref.shape[0]

    @pl.when(j == n_steps - 1)
    def _():
        rr = lax.broadcasted_iota(jnp.int32, (n_rows, W_B), 0) % H_B
        cc = lax.broadcasted_iota(jnp.int32, (n_rows, W_B), 1) // DH_B
        own = jnp.where(rr == cc, acc_ref[...], 0.0)
        o_ref[0] = jnp.sum(own.reshape(n_q, H_B, W_B), axis=1)


def _sb_kernel(pt_ref, bias_ref, q_ref, k_ref, v_ref, qbd_ref, dbias_ref, knew_ref, vnew_ref,
               pool_k_ref, pool_v_ref, o_ref, od_ref, acc_ref, carry_ref, dacc_ref, dcarry_ref,
               kbuf_ref, vbuf_ref, sem_ref, *, blk, pages_per_step, steps_per_seq, n_pages, n_q):
    step = pl.program_id(0) * pl.num_programs(1) + pl.program_id(1)
    n_steps = pl.num_programs(0) * pl.num_programs(1)
    slot = step % 2
    j = step % steps_per_seq

    def page_copy(pool_ref, buf_ref, page, dst_slot, p):
        return pltpu.make_async_copy(pool_ref.at[page], buf_ref.at[dst_slot, p], sem_ref.at[dst_slot])

    def start_pages(t, dst_slot):
        seq, jt = t // steps_per_seq, t % steps_per_seq
        for p in range(pages_per_step):
            page = pt_ref[seq, n_pages - 1 - (jt * pages_per_step + p)]
            page_copy(pool_k_ref, kbuf_ref, page, dst_slot, p).start()
            page_copy(pool_v_ref, vbuf_ref, page, dst_slot, p).start()

    @pl.when(step == 0)
    def _():
        start_pages(step, slot)

    @pl.when(step + 1 < n_steps)
    def _():
        start_pages(step + 1, 1 - slot)

    for p in range(pages_per_step):
        page_copy(pool_k_ref, kbuf_ref, 0, slot, p).wait()
        page_copy(pool_v_ref, vbuf_ref, 0, slot, p).wait()
    k_refs = [kbuf_ref.at[slot, pl.ds(p, 1)] for p in range(pages_per_step)]
    v_refs = [vbuf_ref.at[slot, pl.ds(p, 1)] for p in range(pages_per_step)]
    _decode_begin(j, qbd_ref, dbias_ref, knew_ref, vnew_ref, dacc_ref, dcarry_ref)
    decode = _decode_chains(qbd_ref, dbias_ref, k_refs, v_refs, dacc_ref, dcarry_ref)
    _attn_step(pl.program_id(1), bias_ref, q_ref, k_ref, v_ref, o_ref, acc_ref, carry_ref, blk, decode)
    _decode_end(j, steps_per_seq, od_ref, dacc_ref, n_q)


def _sb_attention(q_hm, k_hm, v_hm, bias2, page_table, qbd, bias_tile, k_new, v_new, pool_k, pool_v,
                  *, blk):
    batch, n_pairs, seq_len, _ = q_hm.shape
    nq = seq_len // blk
    n_heads = 2 * n_pairs
    n_seq, n_pages = page_table.shape
    n_rows = qbd.shape[1]
    n_q = n_rows // H_B
    n_steps = batch * nq
    pages_per_step = n_seq * n_pages // n_steps
    steps_per_seq = n_pages // pages_per_step
    assert pages_per_step * n_steps == n_seq * n_pages and steps_per_seq * pages_per_step == n_pages
    assert pages_per_step % DECODE_GROUP == 0

    seq_of = lambda b, i: (b * nq + i) // steps_per_seq
    per_seq = lambda shape: pl.BlockSpec((1,) + shape, lambda b, i, pt: (seq_of(b, i), 0, 0))
    kv_spec = pl.BlockSpec((1, n_pairs, seq_len, LANES), lambda b, i, pt: (b, 0, 0, 0),
                           pipeline_mode=pl.Buffered(1))
    grid_spec = pltpu.PrefetchScalarGridSpec(
        num_scalar_prefetch=1, grid=(batch, nq),
        in_specs=[pl.BlockSpec((n_heads, LANES), lambda b, i, pt: (0, 0)),
                  pl.BlockSpec((1, n_pairs, blk, LANES), lambda b, i, pt: (b, 0, i, 0)),
                  kv_spec, kv_spec,
                  per_seq((n_rows, W_B)),
                  pl.BlockSpec((n_rows, LANES), lambda b, i, pt: (0, 0)),
                  per_seq((W_B, PAGE_SIZE)), per_seq((W_B, PAGE_SIZE)),
                  pl.BlockSpec(memory_space=pl.ANY), pl.BlockSpec(memory_space=pl.ANY)],
        out_specs=[pl.BlockSpec((1, blk, n_pairs * LANES), lambda b, i, pt: (b, i, 0)),
                   per_seq((n_q, W_B))],
        scratch_shapes=[pltpu.VMEM((n_heads, blk, LANES), F32), pltpu.VMEM((n_heads, blk, LANES), F32),
                        pltpu.VMEM((n_rows, W_B), F32), pltpu.VMEM((n_rows, LANES), F32),
                        pltpu.VMEM((2, pages_per_step, W_B, PAGE_SIZE), F32),
                        pltpu.VMEM((2, pages_per_step, W_B, PAGE_SIZE), F32),
                        pltpu.SemaphoreType.DMA((2,))])
    return pl.pallas_call(
        functools.partial(_sb_kernel, blk=blk, pages_per_step=pages_per_step,
                          steps_per_seq=steps_per_seq, n_pages=n_pages, n_q=n_q),
        grid_spec=grid_spec,
        out_shape=[jax.ShapeDtypeStruct((batch, seq_len, n_pairs * LANES), BF16),
                   jax.ShapeDtypeStruct((n_seq, n_q, W_B), F32)],
        compiler_params=_cparams(2), name="sb_attention",
    )(page_table, _bias_rows(bias2), q_hm, k_hm, v_hm, qbd, bias_tile, k_new, v_new, pool_k, pool_v)


def _mixer_out_ln(x_ref, a1_ref, a2_ref, wo_ref, g_ref, b_ref):
    a = jnp.concatenate([a1_ref[...], a2_ref[...]], axis=1)
    y = jnp.dot(a, wo_ref[...], preferred_element_type=F32)
    return _ln(ALPHA * x_ref[...] + y, g_ref[...], b_ref[...])


def _mixer_specs(tm, row_map, k1, a2_block_col, w_shape):
    i0 = lambda *idx: (row_map(*idx), 0)
    i2 = lambda *idx: (row_map(*idx), a2_block_col)
    return [pl.BlockSpec((tm, D_MODEL), i0), pl.BlockSpec((tm, k1), i0), pl.BlockSpec((tm, k1), i2),
            _const_spec(w_shape), _const_spec((1, D_MODEL)), _const_spec((1, D_MODEL))]


def _ffn_chunk(x_bf, wup_ref, cw_ref, cb_ref, ci, shift_fix):
    cols = slice(ci * FF_CHUNK, (ci + 1) * FF_CHUNK)
    vcols = slice(D_FF + ci * FF_CHUNK, D_FF + (ci + 1) * FF_CHUNK)
    g = jnp.dot(x_bf, wup_ref[:, cols], preferred_element_type=F32)
    val = jnp.dot(x_bf, wup_ref[:, vcols], preferred_element_type=F32)
    g1, g2 = shift_fix(g, cols)
    cw = cw_ref[:, cols]
    conv = cb_ref[:, cols] + cw[0:1] * g2 + cw[1:2] * g1 + cw[2:3] * g
    return g, (jax.nn.gelu(conv) * val).astype(BF16)


def _ffn_prompt_kernel(x_ref, a1_ref, a2_ref, wo_ref, g1_ref, b1_ref,
                       wup_ref, cw_ref, cb_ref, wdn_ref, g_ref, b_ref,
                       y_ref, conv_ref, act_ref, prev_ref):
    tm = x_ref.shape[0]

    @pl.when(pl.program_id(1) == 0)
    def _():
        prev_ref[...] = jnp.zeros_like(prev_ref)

    x = _mixer_out_ln(x_ref, a1_ref, a2_ref, wo_ref, g1_ref, b1_ref)
    x_bf = x.astype(BF16)
    row = lax.broadcasted_iota(jnp.int32, (tm, FF_CHUNK), 0)

    def shift_fix(g, cols):
        p0 = prev_ref[0:1, cols]
        p1 = prev_ref[1:2, cols]
        g1 = jnp.where(row == 0, p1, pltpu.roll(g, 1, 0))
        g2 = jnp.where(row == 0, p0, jnp.where(row == 1, p1, pltpu.roll(g, 2, 0)))
        return g1, g2

    for ci in range(D_FF // FF_CHUNK):
        cols = slice(ci * FF_CHUNK, (ci + 1) * FF_CHUNK)
        g, act = _ffn_chunk(x_bf, wup_ref, cw_ref, cb_ref, ci, shift_fix)
        act_ref[:, cols] = act
        prev_ref[0:2, cols] = g[tm - 2:tm]
        conv_ref[0, :, cols] = g[tm - 2:tm]
    y = jnp.dot(act_ref[...], wdn_ref[...], preferred_element_type=F32)
    y_ref[...] = _ln(ALPHA * x + y, g_ref[...], b_ref[...])


def _ffn_sample_kernel(x_ref, a1_ref, a2_ref, wo_ref, g1_ref, b1_ref,
                       wup_ref, cw_ref, cb_ref, wdn_ref, g_ref, b_ref, inj1_ref, inj2_ref,
                       y_ref, gate_ref, act_ref, *, seq_len):
    tm = x_ref.shape[0]
    x = _mixer_out_ln(x_ref, a1_ref, a2_ref, wo_ref, g1_ref, b1_ref)
    x_bf = x.astype(BF16)
    pos = lax.broadcasted_iota(jnp.int32, (tm, FF_CHUNK), 0) % seq_len

    def shift_fix(g, cols):
        g1 = jnp.where(pos >= 1, pltpu.roll(g, 1, 0), inj1_ref[:, cols])
        g2 = jnp.where(pos >= 2, pltpu.roll(g, 2, 0), inj2_ref[:, cols])
        return g1, g2

    for ci in range(D_FF // FF_CHUNK):
        cols = slice(ci * FF_CHUNK, (ci + 1) * FF_CHUNK)
        g, act = _ffn_chunk(x_bf, wup_ref, cw_ref, cb_ref, ci, shift_fix)
        act_ref[:, cols] = act
        gate_ref[:, cols] = g
    y = jnp.dot(act_ref[...], wdn_ref[...], preferred_element_type=F32)
    y_ref[...] = _ln(ALPHA * x + y, g_ref[...], b_ref[...])


def _ffn_weight_specs():
    return [_const_spec((D_MODEL, 2 * D_FF)), _const_spec((CONV_W, D_FF)), _const_spec((1, D_FF)),
            _const_spec((D_FF, D_MODEL)), _const_spec((1, D_MODEL)), _const_spec((1, D_MODEL))]


def _ffn_prompt(mixer, wup_bf, cw, cb, wdn_bf, g, b, *, tm, seq_len):
    x2d, a1, a2, a2_col, wo, g1, b1 = mixer
    rows = x2d.shape[0]
    batch = rows // seq_len
    tps = seq_len // tm
    row_map = lambda bi, t: bi * tps + t
    return pl.pallas_call(
        _ffn_prompt_kernel, grid=(batch, tps),
        in_specs=_mixer_specs(tm, row_map, wo.shape[0] // 2, a2_col, wo.shape) + _ffn_weight_specs(),
        out_specs=[pl.BlockSpec((tm, D_MODEL), lambda bi, t: (bi * tps + t, 0)),
                   pl.BlockSpec((1, CONV_W - 1, D_FF), lambda bi, t: (bi, 0, 0))],
        out_shape=[jax.ShapeDtypeStruct((rows, D_MODEL), F32),
                   jax.ShapeDtypeStruct((batch, CONV_W - 1, D_FF), F32)],
        scratch_shapes=[pltpu.VMEM((tm, D_FF), BF16), pltpu.VMEM((8, D_FF), F32)],
        compiler_params=_cparams(2), name="ffn_prompt",
    )(x2d, a1, a2, wo, g1, b1, wup_bf, cw, cb, wdn_bf, g, b)


def _ffn_sample(mixer, wup_bf, cw, cb, wdn_bf, g, b, conv_state, *, seq_len):
    x2d, a1, a2, a2_col, wo, g1, b1 = mixer
    rows = x2d.shape[0]
    n_seq = rows // seq_len
    zeros = jnp.zeros((n_seq, 1, D_FF), F32)
    inj1 = jnp.concatenate([conv_state[:, 1:2]] + [zeros] * (seq_len - 1), axis=1).reshape(rows, D_FF)
    inj2 = jnp.concatenate([conv_state] + [zeros] * (seq_len - 2), axis=1).reshape(rows, D_FF)
    full = lambda width: pl.BlockSpec((rows, width), lambda i: (0, 0))
    y, gate = pl.pallas_call(
        functools.partial(_ffn_sample_kernel, seq_len=seq_len), grid=(1,),
        in_specs=(_mixer_specs(rows, lambda i: 0, wo.shape[0] // 2, a2_col, wo.shape)
                  + _ffn_weight_specs() + [full(D_FF), full(D_FF)]),
        out_specs=[full(D_MODEL), full(D_FF)],
        out_shape=[jax.ShapeDtypeStruct((rows, D_MODEL), F32), jax.ShapeDtypeStruct((rows, D_FF), F32)],
        scratch_shapes=[pltpu.VMEM((rows, D_FF), BF16)],
        compiler_params=_cparams(1), name="ffn_sample",
    )(x2d, a1, a2, wo, g1, b1, wup_bf, cw, cb, wdn_bf, g, b, inj1, inj2)
    new_state = gate.reshape(n_seq, seq_len, D_FF)[:, seq_len - (CONV_W - 1):]
    return y, new_state


def _ret_kernel(x_ref, w_ref, cos_ref, sin_ref, gg_ref, gb_ref, *refs, decay_len, has_state):
    if has_state:
        s0_ref, y_ref, sout_ref, s_ref = refs
    else:
        y_ref, sout_ref, s_ref = refs
    tm = x_ref.shape[0]
    t = pl.program_id(1)

    @pl.when(t == 0)
    def _():
        if has_state:
            s_ref[...] = s0_ref[0]
        else:
            s_ref[...] = jnp.zeros_like(s_ref)

    x = x_ref[...].astype(BF16)
    cos = cos_ref[...]
    sin = sin_ref[...]
    half = DK_C // 2

    def rotary(a):
        a1, a2 = a[:, :half], a[:, half:]
        return jnp.concatenate([a1 * cos - a2 * sin, a1 * sin + a2 * cos], axis=1)

    ii = lax.broadcasted_iota(jnp.int32, (RET_CHUNK, RET_CHUNK), 0)
    jj = lax.broadcasted_iota(jnp.int32, (RET_CHUNK, RET_CHUNK), 1)
    diff = jnp.where(ii >= jj, ii - jj, 0).astype(F32)
    idx = lax.broadcasted_iota(jnp.int32, (RET_CHUNK, 1), 0).astype(F32)
    idx_tile = (lax.broadcasted_iota(jnp.int32, (tm, 1), 0) % RET_CHUNK).astype(F32)
    v_off = 2 * H_C * DK_C
    g_off = v_off + H_C * DV_C
    n_chunks = tm // RET_CHUNK
    heads = [dict() for _ in range(H_C)]

    def project(h):
        q = rotary(jnp.dot(x, w_ref[:, h * DK_C:(h + 1) * DK_C], preferred_element_type=F32))
        k = rotary(jnp.dot(x, w_ref[:, (H_C + h) * DK_C:(H_C + h + 1) * DK_C],
                           preferred_element_type=F32)) * (DK_C ** -0.5)
        v = jnp.dot(x, w_ref[:, v_off + h * DV_C:v_off + (h + 1) * DV_C], preferred_element_type=F32)
        k_dec = jnp.exp((decay_len - 1.0 - idx_tile) * _LOG_G[h])
        heads[h].update(q=q.astype(BF16), k=k.astype(BF16), v=v.astype(BF16),
                        kd_t=(k * k_dec).T.astype(BF16))

    def project_gate(h):
        heads[h]["gate"] = jnp.dot(x, w_ref[:, g_off + h * DV_C:g_off + (h + 1) * DV_C],
                                   preferred_element_type=F32)

    chunks = [slice(ci * RET_CHUNK, (ci + 1) * RET_CHUNK) for ci in range(n_chunks)]

    def advance_state(h):
        c_dec = math.exp(decay_len * _LOG_G[h])
        v, kd_t = heads[h]["v"], heads[h].pop("kd_t")
        kvs = [jnp.dot(kd_t[:, rows], v[rows], preferred_element_type=F32) for rows in chunks]
        states = [s_ref[h]]
        for kv in kvs:
            states.append(states[-1] * c_dec + kv)
        s_ref[h] = states[-1]
        heads[h]["states"] = [state.astype(BF16) for state in states[:-1]]

    def retain(h):
        log_g = _LOG_G[h]
        dmat = jnp.where(ii >= jj, jnp.exp(diff * log_g), 0.0)
        q_dec = jnp.exp((idx + 1.0) * log_g)
        q, k, v, states = (heads[h].pop(name) for name in ("q", "k", "v", "states"))
        outs = [None] * n_chunks
        vals = [dict() for _ in chunks]

        def scores(ci):
            vals[ci]["sc"] = lax.dot_general(q[chunks[ci]], k[chunks[ci]], (((1,), (1,)), ((), ())),
                                             preferred_element_type=F32)

        def decay(ci):
            vals[ci]["sc"] = (vals[ci]["sc"] * dmat).astype(BF16)

        def output(ci):
            rows = chunks[ci]
            outs[ci] = (jnp.dot(vals[ci].pop("sc"), v[rows], preferred_element_type=F32)
                        + jnp.dot(q[rows], states[ci], preferred_element_type=F32) * q_dec)

        _emit_skewed([[functools.partial(stage, ci) for stage in (scores, decay, output)]
                      for ci in range(n_chunks)])
        heads[h]["o"] = outs

    def finish(h):
        vcols = slice(h * DV_C, (h + 1) * DV_C)
        gate = heads[h].pop("gate")
        for ci, o in enumerate(heads[h].pop("o")):
            rows = slice(ci * RET_CHUNK, (ci + 1) * RET_CHUNK)
            mu = jnp.mean(o, axis=-1, keepdims=True)
            oc = o - mu
            var = jnp.mean(oc * oc, axis=-1, keepdims=True)
            normed = oc * lax.rsqrt(var + LN_EPS) * gg_ref[:, vcols] + gb_ref[:, vcols]
            y_ref[rows, vcols] = (jax.nn.silu(gate[rows]) * normed).astype(y_ref.dtype)

    project(0)
    for h in range(H_C):
        advance_state(h)
        if h + 1 < H_C:
            project(h + 1)
        retain(h)
        project_gate(h)
        if h > 0:
            finish(h - 1)
    finish(H_C - 1)

    @pl.when(t == pl.num_programs(1) - 1)
    def _():
        sout_ref[0] = s_ref[...]


def _retention(x2d, w_bf, cos, sin, gg, gb, state0, *, tm, seq_len, decay_len, name):
    rows = x2d.shape[0]
    n_seq = rows // seq_len
    tps = seq_len // tm
    d_y = H_C * DV_C
    has_state = state0 is not None
    state_spec = pl.BlockSpec((1, H_C, DK_C, DV_C), lambda bi, t: (bi, 0, 0, 0))
    in_specs = [pl.BlockSpec((tm, D_MODEL), lambda bi, t: (bi * tps + t, 0)),
                _const_spec(w_bf.shape),
                pl.BlockSpec((tm, DK_C // 2), lambda bi, t: (t, 0)),
                pl.BlockSpec((tm, DK_C // 2), lambda bi, t: (t, 0)),
                _const_spec((1, d_y)), _const_spec((1, d_y))]
    args = [x2d, w_bf, cos, sin, gg, gb]
    if has_state:
        in_specs.append(state_spec)
        args.append(state0)
    return pl.pallas_call(
        functools.partial(_ret_kernel, decay_len=decay_len, has_state=has_state),
        grid=(n_seq, tps), in_specs=in_specs,
        out_specs=[pl.BlockSpec((tm, d_y), lambda bi, t: (bi * tps + t, 0)), state_spec],
        out_shape=[jax.ShapeDtypeStruct((rows, d_y), BF16),
                   jax.ShapeDtypeStruct((n_seq, H_C, DK_C, DV_C), F32)],
        scratch_shapes=[pltpu.VMEM((H_C, DK_C, DV_C), F32)],
        compiler_params=_cparams(2), name=name,
    )(*args)


def _rope_tables(pos):
    half = DK_C // 2
    inv = ROPE_BASE ** (-jnp.arange(half, dtype=F32) / half)
    ang = pos.astype(F32)[:, None] * inv[None, :]
    return jnp.cos(ang), jnp.sin(ang)


def kernel(x_prompt, x_sample, cache_sb_k, cache_sb_v, state_ret, state_ffn_conv, page_table,
           w_in_ab, vln_g, vln_b, w_s, b_s, sb_bias, w_out_ab, w_in_ret, gn_g, gn_b, w_out_ret,
           ln1_g, ln1_b, ln2_g, ln2_b, w_up, conv_w, conv_b, w_down):
    bp, tp, _ = x_prompt.shape
    bs, ts, _ = x_sample.shape
    n_pages = page_table.shape[1]
    past_len = n_pages * PAGE_SIZE
    n_phys = cache_sb_k.shape[1]
    xp = x_prompt.reshape(bp * tp, D_MODEL)
    xs = x_sample.reshape(bs * ts, D_MODEL)
    row2d = lambda a: a.reshape(1, -1)

    sb_k_p, sb_v_p, sb_k_s, sb_v_s, chunk_v_s = [], [], [], [], []
    ret_p, ret_s, conv_p, conv_s = [], [], [], []
    for l in range(DEPTH):
        if l % 2 == 0:
            e = l // 2
            w_in = w_in_ab[e].astype(BF16)
            vg, vb = row2d(vln_g[e]), row2d(vln_b[e])
            mixb_p = jnp.repeat(b_s[e].T, DG_A, axis=1)
            a_p, q_hm, k_hm, v_hm, kf_p, vf_p = _ab_proj(
                xp, w_in, vg, vb, w_s[e], mixb_p, mix_len=CHUNK, head_major=True, tm=512, seq_len=tp)
            bias2 = sb_bias[e] * LOG2E

            reps = CHUNK // ts
            wmix_s = jnp.tile(w_s[e][:, :ts, :ts], (1, reps, reps))
            mixb_s = jnp.repeat(jnp.tile(b_s[e][:, :ts], (1, reps)).T, DG_A, axis=1)
            a_s, q_s, kf_s, vf_s, va_s = _ab_proj(
                xs, w_in, vg, vb, wmix_s, mixb_s, mix_len=ts, head_major=False, tm=bs * ts)
            head_of_col = jnp.arange(W_B) // DH_B
            own = (jnp.arange(H_B)[:, None] == head_of_col[None, :]).astype(F32)
            qbd = (q_s.reshape(bs, ts, 1, W_B) * own[None, None]).reshape(bs, ts * H_B, W_B).astype(BF16)
            bias_tile = jnp.broadcast_to(jnp.tile(bias2, ts)[:, None], (ts * H_B, LANES))
            pad = ((0, 0), (0, 0), (0, PAGE_SIZE - ts))
            k_new = jnp.pad(kf_s.reshape(bs, ts, W_B).transpose(0, 2, 1), pad)
            v_new = jnp.pad(vf_s.reshape(bs, ts, W_B).transpose(0, 2, 1), pad)
            pool_k = cache_sb_k[e].transpose(0, 2, 3, 1).reshape(n_phys, W_B, PAGE_SIZE)
            pool_v = cache_sb_v[e].transpose(0, 2, 3, 1).reshape(n_phys, W_B, PAGE_SIZE)
            b_p, b_s_out = _sb_attention(q_hm, k_hm, v_hm, bias2, page_table, qbd, bias_tile,
                                         k_new, v_new, pool_k, pool_v, blk=256)

            w_out = w_out_ab[e].astype(BF16)
            g1, b1 = row2d(ln1_g[l]), row2d(ln1_b[l])
            mix_p = (xp, a_p, b_p.reshape(bp * tp, W_B), 0, w_out, g1, b1)
            mix_s = (xs, a_s, b_s_out.reshape(bs * ts, W_B).astype(BF16), 0, w_out, g1, b1)
            pages_of = lambda a: a.reshape(bp, tp // PAGE_SIZE, H_B, DH_B, PAGE_SIZE).transpose(0, 1, 4, 2, 3)
            sb_k_p.append(pages_of(kf_p))
            sb_v_p.append(pages_of(vf_p))
            sb_k_s.append(kf_s.reshape(bs, ts, H_B, DH_B))
            sb_v_s.append(vf_s.reshape(bs, ts, H_B, DH_B))
            chunk_v_s.append(va_s.reshape(bs, ts, G_A, DG_A))
        else:
            o = l // 2
            w_in = w_in_ret[o].astype(BF16)
            gg, gb = row2d(gn_g[o]), row2d(gn_b[o])
            cos_p, sin_p = _rope_tables(jnp.arange(tp))
            y_p, s_p = _retention(xp, w_in, cos_p, sin_p, gg, gb, None, tm=512, seq_len=tp,
                                  decay_len=RET_CHUNK, name="retention_prompt")
            cos_s, sin_s = _rope_tables(past_len + jnp.arange(RET_CHUNK))
            xs_pad = jnp.pad(xs.reshape(bs, ts, D_MODEL), ((0, 0), (0, RET_CHUNK - ts), (0, 0)))
            y_s, s_s = _retention(xs_pad.reshape(bs * RET_CHUNK, D_MODEL), w_in, cos_s, sin_s, gg, gb,
                                  state_ret[o], tm=RET_CHUNK, seq_len=RET_CHUNK, decay_len=ts,
                                  name="retention_sample")
            y_s = y_s.reshape(bs, RET_CHUNK, H_C * DV_C)[:, :ts].reshape(bs * ts, H_C * DV_C)
            w_out = w_out_ret[o].astype(BF16)
            g1, b1 = row2d(ln1_g[l]), row2d(ln1_b[l])
            mix_p = (xp, y_p, y_p, 1, w_out, g1, b1)
            mix_s = (xs, y_s, y_s, 1, w_out, g1, b1)
            ret_p.append(s_p)
            ret_s.append(s_s)

        wup = w_up[l].astype(BF16)
        wdn = w_down[l].astype(BF16)
        g2, b2 = row2d(ln2_g[l]), row2d(ln2_b[l])
        xp, cp = _ffn_prompt(mix_p, wup, conv_w[l], row2d(conv_b[l]), wdn, g2, b2, tm=512, seq_len=tp)
        xs, cs = _ffn_sample(mix_s, wup, conv_w[l], row2d(conv_b[l]), wdn, g2, b2, state_ffn_conv[l], seq_len=ts)
        conv_p.append(cp)
        conv_s.append(cs)

    return (xp.reshape(bp, tp, D_MODEL), xs.reshape(bs, ts, D_MODEL),
            jnp.stack(sb_k_p), jnp.stack(sb_v_p), jnp.stack(sb_k_s), jnp.stack(sb_v_s),
            jnp.stack(chunk_v_s), jnp.stack(ret_p), jnp.stack(ret_s), jnp.stack(conv_p), jnp.stack(conv_s))
```

```python
import functools
import math

import jax
import jax.numpy as jnp
from jax import lax
from jax.experimental import pallas as pl
from jax.experimental.pallas import tpu as pltpu

F32 = jnp.float32
BF16 = jnp.bfloat16

D_MODEL = 1024
W_A = D_MODEL // 2
G_A = 8
DG_A = W_A // G_A
CHUNK = 128
W_B = D_MODEL // 2
H_B = 8
DH_B = W_B // H_B
PAGE_SIZE = 128
H_C = 4
DK_C = D_MODEL // H_C
DV_C = 2 * DK_C
RET_CHUNK = 128
ROPE_BASE = 10000.0
D_FF = ((8 * D_MODEL // 3 + 127) // 128) * 128
CONV_W = 3
DEPTH = 2
ALPHA = (2 * DEPTH) ** 0.25
LN_EPS = 1e-5
LOG2E = 1.4426950408889634
EXP2_CLAMP = 126.0
BIAS_PIECES = 3
DECODE_GROUP = 16
LOOP_BLOCKS = 4

LANES = 128
FF_CHUNK = 256
VMEM_LIMIT = 56 * 1024 * 1024
ROW_TILE = 512
ATTN_BLOCK = 256

_LOG_G = tuple(math.log1p(-(2.0 ** (-5.0 - h))) for h in range(H_C))


def _cparams(n_axes):
    return pltpu.CompilerParams(
        dimension_semantics=("arbitrary",) * n_axes, vmem_limit_bytes=VMEM_LIMIT)


def _const_spec(shape):
    nd = len(shape)
    return pl.BlockSpec(shape, lambda *_: (0,) * nd, pipeline_mode=pl.Buffered(1))


def _ln(x, g, b):
    mu = jnp.mean(x, axis=-1, keepdims=True)
    xc = x - mu
    var = jnp.mean(xc * xc, axis=-1, keepdims=True)
    return xc * lax.rsqrt(var + LN_EPS) * g + b


def _softplus2(w):
    return jnp.maximum(w, jnp.log2(1.0 + jnp.exp2(jnp.minimum(w, EXP2_CLAMP))))


def _split_bf16(x):
    hi = x.astype(BF16)
    lo = (x - hi.astype(F32)).astype(BF16)
    return jnp.concatenate([hi, lo], axis=1)


def _emit_skewed(chains):
    depth = max(len(chain) for chain in chains)
    for t in range(len(chains) + depth - 1):
        for s in range(depth):
            if 0 <= t - s < len(chains) and s < len(chains[t - s]):
                chains[t - s][s]()


def _neg_suffix_matrix(n, parts, inclusive):
    j = lax.broadcasted_iota(jnp.int32, (parts * n, n), 0) % n
    s = lax.broadcasted_iota(jnp.int32, (parts * n, n), 1)
    return jnp.where((j >= s) if inclusive else (j > s), -1.0, 0.0).astype(BF16)


def _ab_proj_kernel(x_ref, w_ref, vg_ref, vb_ref, wmix_ref, mixb_ref, *refs, mix_len, head_major):
    if head_major:
        a_ref, q_ref, k_ref, v_ref, kf_ref, vf_ref, mixed_ref = refs
    else:
        a_ref, q_ref, kf_ref, vf_ref, va_ref, mixed_ref = refs
    tm = x_ref.shape[0]
    x = x_ref[...].astype(BF16)
    h = jnp.dot(x, w_ref[...], preferred_element_type=F32)
    u = jax.nn.gelu(h[:, :W_A])
    va = _ln(jax.nn.gelu(h[:, W_A:2 * W_A]), vg_ref[...], vb_ref[...])
    q = h[:, 2 * W_A:2 * W_A + W_B] * (DH_B ** -0.5 * LOG2E)
    k = h[:, 2 * W_A + W_B:2 * W_A + 2 * W_B]
    v = h[:, 2 * W_A + 2 * W_B:]
    if head_major:
        for p in range(W_B // LANES):
            sl = slice(p * LANES, (p + 1) * LANES)
            q_ref[0, p] = q[:, sl].astype(BF16)
            k_ref[0, p] = k[:, sl].astype(BF16)
            v_ref[0, p] = v[:, sl].astype(BF16)
        for pg in range(tm // PAGE_SIZE):
            rows = slice(pg * PAGE_SIZE, (pg + 1) * PAGE_SIZE)
            kf_ref[pg] = k[rows].T
            vf_ref[pg] = v[rows].T
    else:
        kf_ref[...] = k
        vf_ref[...] = v
        q_ref[...] = q
        va_ref[...] = va

    r = lax.broadcasted_iota(jnp.int32, (CHUNK, CHUNK), 0)
    c = lax.broadcasted_iota(jnp.int32, (CHUNK, CHUNK), 1)
    if mix_len == CHUNK:
        mask = c <= r
    else:
        mask = jnp.logical_and(r // mix_len == c // mix_len, c % mix_len <= r % mix_len)
    wg = [jnp.where(mask, wmix_ref[g], 0.0).astype(BF16) for g in range(G_A)]
    for ci in range(tm // CHUNK):
        rows = slice(ci * CHUNK, (ci + 1) * CHUNK)
        va_c = va[rows]
        for g in range(G_A):
            cols = slice(g * DG_A, (g + 1) * DG_A)
            mixed_ref[rows, cols] = jnp.dot(wg[g], va_c[:, cols].astype(BF16),
                                            preferred_element_type=F32)
        a_ref[rows, :] = (u[rows] * (mixed_ref[rows, :] + mixb_ref[...])).astype(BF16)


def _ab_proj(x2d, w_bf, vg, vb, wmix, mixb, *, mix_len, head_major, tm, seq_len=None):
    rows = x2d.shape[0]
    d_in = w_bf.shape[1]
    n_tiles = rows // tm
    row_spec = lambda width: pl.BlockSpec((tm, width), lambda i: (i, 0))
    in_specs = [row_spec(D_MODEL), _const_spec((D_MODEL, d_in)), _const_spec((1, W_A)),
                _const_spec((1, W_A)), _const_spec((G_A, CHUNK, CHUNK)), _const_spec((CHUNK, W_A))]
    if head_major:
        tps = seq_len // tm
        batch = rows // seq_len
        hm_spec = pl.BlockSpec((1, W_B // LANES, tm, LANES), lambda i: (i // tps, 0, i % tps, 0))
        hm_shape = jax.ShapeDtypeStruct((batch, W_B // LANES, seq_len, LANES), BF16)
        ppt = tm // PAGE_SIZE
        pg_spec = pl.BlockSpec((ppt, W_B, PAGE_SIZE), lambda i: (i, 0, 0))
        pg_shape = jax.ShapeDtypeStruct((rows // PAGE_SIZE, W_B, PAGE_SIZE), F32)
        out_specs = [row_spec(W_A), hm_spec, hm_spec, hm_spec, pg_spec, pg_spec]
        out_shape = [jax.ShapeDtypeStruct((rows, W_A), BF16), hm_shape, hm_shape, hm_shape,
                     pg_shape, pg_shape]
    else:
        out_specs = [row_spec(W_A), row_spec(W_B), row_spec(W_B), row_spec(W_B), row_spec(W_A)]
        out_shape = [jax.ShapeDtypeStruct((rows, W_A), BF16)] + [
            jax.ShapeDtypeStruct((rows, W_B), F32)] * 3 + [jax.ShapeDtypeStruct((rows, W_A), F32)]
    return pl.pallas_call(
        functools.partial(_ab_proj_kernel, mix_len=mix_len, head_major=head_major),
        grid=(n_tiles,), in_specs=in_specs, out_specs=out_specs, out_shape=out_shape,
        scratch_shapes=[pltpu.VMEM((tm, W_A), F32)],
        compiler_params=_cparams(1),
        name="ab_proj_prompt" if head_major else "ab_proj_sample",
    )(x2d, w_bf, vg, vb, wmix, mixb)


def _attn_step(qi, bias_ref, q_ref, k_ref, v_ref, o_ref, acc_ref, carry_ref, blk, extra_chains):
    n_pairs = q_ref.shape[1]
    n_heads = 2 * n_pairs
    lane = lax.broadcasted_iota(jnp.int32, (blk, LANES), 1)
    neg_u = _neg_suffix_matrix(blk, 1, inclusive=False)
    row = lax.broadcasted_iota(jnp.int32, (blk, blk), 0)
    col = lax.broadcasted_iota(jnp.int32, (blk, blk), 1)
    causal = col < row
    k_ones = jnp.where(lane < BIAS_PIECES, 1.0, 0.0).astype(BF16)
    q_aug = []
    for h in range(n_heads):
        q2 = q_ref[0, h // 2]
        qh = jnp.where((lane >= DH_B) == bool(h % 2), q2, jnp.zeros_like(q2))
        q_aug.append(jnp.concatenate(
            [qh, jnp.broadcast_to(bias_ref[h:h + 1, :], (blk, LANES))], axis=1))

    def start_of(kb):
        return kb * blk if isinstance(kb, int) else pl.multiple_of(kb * blk, blk)

    def block_chains(kb, diagonal):
        start = start_of(kb)
        vals = [dict() for _ in range(n_heads)]

        def logits(h):
            k2 = k_ref[0, h // 2, pl.ds(start, blk), :]
            vals[h]["w"] = lax.dot_general(q_aug[h], jnp.concatenate([k2, k_ones], axis=1),
                                           (((1,), (1,)), ((), ())), preferred_element_type=F32)

        def keep(h):
            w = vals[h].pop("w")
            sp = _softplus2(w)
            if diagonal:
                sp = jnp.where(causal, sp, 0.0)
                vals[h]["d"] = w - sp
                carry_ref[h] = jnp.broadcast_to(-jnp.sum(sp, axis=1, keepdims=True), (blk, LANES))
            else:
                carry = carry_ref[h]
                vals[h]["d"] = w - sp + jnp.concatenate([carry] * (blk // LANES), axis=1)
                carry_ref[h] = carry - jnp.sum(sp, axis=1, keepdims=True)
            vals[h]["sp"] = sp.astype(BF16)

        def suffix(h):
            vals[h]["suf"] = jnp.dot(vals[h].pop("sp"), neg_u, preferred_element_type=F32)

        def weights(h):
            a = jnp.exp2(vals[h].pop("d") + vals[h].pop("suf"))
            if diagonal:
                a = jnp.where(causal, a, 0.0)
            vals[h]["a"] = a.astype(BF16)

        def values(h):
            v2 = v_ref[0, h // 2, pl.ds(start, blk), :]
            pv = jnp.dot(vals[h].pop("a"), v2, preferred_element_type=F32)
            if diagonal:
                acc_ref[h] = pv
            else:
                acc_ref[h] += pv

        return [[functools.partial(stage, h) for stage in (logits, keep, suffix, weights, values)]
                for h in range(n_heads)]

    def blocks(kbs, diagonal):
        _emit_skewed([chain for kb in kbs for chain in block_chains(kb, diagonal)])

    diag = block_chains(qi, True)
    every = max(1, len(diag) // max(1, len(extra_chains)))
    merged = []
    for i, chain in enumerate(diag):
        if i % every == 0 and i // every < len(extra_chains):
            merged.append(extra_chains[i // every])
        merged.append(chain)
    _emit_skewed(merged)

    def trip(j, _):
        blocks([qi - 1 - LOOP_BLOCKS * j - i for i in range(LOOP_BLOCKS)], False)
        return 0

    lax.fori_loop(0, qi // LOOP_BLOCKS, trip, 0)
    for left in range(1, LOOP_BLOCKS):
        pl.when(qi % LOOP_BLOCKS == left)(
            functools.partial(blocks, list(range(left - 1, -1, -1)), False))

    for hp in range(n_pairs):
        o_ref[0, :, hp * LANES:(hp + 1) * LANES] = jnp.where(
            lane < DH_B, acc_ref[2 * hp], acc_ref[2 * hp + 1]).astype(o_ref.dtype)


def _bias_rows(bias2):
    pieces, rest = [], bias2.astype(F32)
    for _ in range(BIAS_PIECES):
        piece = rest.astype(BF16)
        pieces.append(piece)
        rest = rest - piece.astype(F32)
    rows = jnp.stack(pieces, axis=1)
    return jnp.pad(rows, ((0, 0), (0, LANES - BIAS_PIECES)))


def _decode_chains(qbd_ref, bias_ref, k_pgs, v_pgs, acc_ref, carry_ref, valid=None):
    n_rows = qbd_ref.shape[1]
    neg_u = _neg_suffix_matrix(PAGE_SIZE, 2, inclusive=True)

    def chain(k_grp, v_grp):
        n = len(k_grp)
        vals = {}

        def logits():
            k_cat = jnp.concatenate([k[0].astype(BF16) for k in k_grp], axis=1)
            vals["w_cat"] = jnp.dot(qbd_ref[0], k_cat, preferred_element_type=F32)

        def keep():
            w_cat = vals.pop("w_cat")
            w = jnp.concatenate([w_cat[:, p * PAGE_SIZE:(p + 1) * PAGE_SIZE] for p in range(n)], axis=0)
            w = w + jnp.concatenate([bias_ref[...]] * n, axis=0)
            sp = _softplus2(w)
            if valid is not None:
                sp = jnp.where(valid, sp, 0.0)
            tot = jnp.sum(sp, axis=1, keepdims=True)
            carry = carry_ref[...]
            carries = []
            for p in range(n):
                carries.append(carry)
                carry = carry - tot[p * n_rows:(p + 1) * n_rows]
            carry_ref[...] = carry
            vals["d"] = w + jnp.concatenate(carries, axis=0)
            vals["split"] = _split_bf16(sp)

        def suffix():
            vals["suf"] = jnp.dot(vals.pop("split"), neg_u, preferred_element_type=F32)

        def weights():
            a = jnp.exp2(vals.pop("d") + vals.pop("suf"))
            if valid is not None:
                a = jnp.where(valid, a, 0.0)
            a = a.astype(BF16)
            vals["a_cat"] = jnp.concatenate([a[p * n_rows:(p + 1) * n_rows] for p in range(n)], axis=1)

        def values():
            v_cat = jnp.concatenate([v[0].astype(BF16) for v in v_grp], axis=1)
            acc_ref[...] += lax.dot_general(vals.pop("a_cat"), v_cat, (((1,), (1,)), ((), ())),
                                            preferred_element_type=F32)

        return [logits, keep, suffix, weights, values]

    return [chain(k_pgs[first:first + DECODE_GROUP], v_pgs[first:first + DECODE_GROUP])
            for first in range(0, len(k_pgs), DECODE_GROUP)]


def _decode_begin(j, qbd_ref, bias_ref, knew_ref, vnew_ref, acc_ref, carry_ref):
    n_rows = qbd_ref.shape[1]

    @pl.when(j == 0)
    def _():
        acc_ref[...] = jnp.zeros_like(acc_ref)
        carry_ref[...] = jnp.zeros_like(carry_ref)
        t = lax.broadcasted_iota(jnp.int32, (n_rows, PAGE_SIZE), 0) // H_B
        i = lax.broadcasted_iota(jnp.int32, (n_rows, PAGE_SIZE), 1)
        _emit_skewed(_decode_chains(qbd_ref, bias_ref, [knew_ref], [vnew_ref], acc_ref, carry_ref,
                                    valid=i < t))


def _decode_end(j, n_steps, o_ref, acc_ref, n_q):
    n_rows = acc_ref.shape[0]

    @pl.when(j == n_steps - 1)
    def _():
        rr = lax.broadcasted_iota(jnp.int32, (n_rows, W_B), 0) % H_B
        cc = lax.broadcasted_iota(jnp.int32, (n_rows, W_B), 1) // DH_B
        own = jnp.where(rr == cc, acc_ref[...], 0.0)
        o_ref[0] = jnp.sum(own.reshape(n_q, H_B, W_B), axis=1)


def _sb_kernel(pt_ref, bias_ref, q_ref, k_ref, v_ref, qbd_ref, dbias_ref, knew_ref, vnew_ref,
               pool_k_ref, pool_v_ref, o_ref, od_ref, acc_ref, carry_ref, dacc_ref, dcarry_ref,
               kbuf_ref, vbuf_ref, sem_ref, *, blk, pages_per_step, steps_per_seq, n_pages, n_q):
    step = pl.program_id(0) * pl.num_programs(1) + pl.program_id(1)
    n_steps = pl.num_programs(0) * pl.num_programs(1)
    slot = step % 2
    j = step % steps_per_seq

    def page_copy(pool_ref, buf_ref, page, dst_slot, p):
        return pltpu.make_async_copy(pool_ref.at[page], buf_ref.at[dst_slot, p], sem_ref.at[dst_slot])

    def start_pages(t, dst_slot):
        seq, jt = t // steps_per_seq, t % steps_per_seq
        for p in range(pages_per_step):
            page = pt_ref[seq, n_pages - 1 - (jt * pages_per_step + p)]
            page_copy(pool_k_ref, kbuf_ref, page, dst_slot, p).start()
            page_copy(pool_v_ref, vbuf_ref, page, dst_slot, p).start()

    @pl.when(step == 0)
    def _():
        start_pages(step, slot)

    @pl.when(step + 1 < n_steps)
    def _():
        start_pages(step + 1, 1 - slot)

    for p in range(pages_per_step):
        page_copy(pool_k_ref, kbuf_ref, 0, slot, p).wait()
        page_copy(pool_v_ref, vbuf_ref, 0, slot, p).wait()
    k_refs = [kbuf_ref.at[slot, pl.ds(p, 1)] for p in range(pages_per_step)]
    v_refs = [vbuf_ref.at[slot, pl.ds(p, 1)] for p in range(pages_per_step)]
    _decode_begin(j, qbd_ref, dbias_ref, knew_ref, vnew_ref, dacc_ref, dcarry_ref)
    decode = _decode_chains(qbd_ref, dbias_ref, k_refs, v_refs, dacc_ref, dcarry_ref)
    _attn_step(pl.program_id(1), bias_ref, q_ref, k_ref, v_ref, o_ref, acc_ref, carry_ref, blk, decode)
    _decode_end(j, steps_per_seq, od_ref, dacc_ref, n_q)


def _sb_attention(q_hm, k_hm, v_hm, bias2, page_table, qbd, bias_tile, k_new, v_new, pool_k, pool_v,
                  *, blk):
    batch, n_pairs, seq_len, _ = q_hm.shape
    nq = seq_len // blk
    n_heads = 2 * n_pairs
    n_seq, n_pages = page_table.shape
    n_rows = qbd.shape[1]
    n_q = n_rows // H_B
    n_steps = batch * nq
    pages_per_step = n_seq * n_pages // n_steps
    steps_per_seq = n_pages // pages_per_step
    assert pages_per_step * n_steps == n_seq * n_pages and steps_per_seq * pages_per_step == n_pages
    assert pages_per_step % DECODE_GROUP == 0

    seq_of = lambda b, i: (b * nq + i) // steps_per_seq
    per_seq = lambda shape: pl.BlockSpec((1,) + shape, lambda b, i, pt: (seq_of(b, i), 0, 0))
    kv_spec = pl.BlockSpec((1, n_pairs, seq_len, LANES), lambda b, i, pt: (b, 0, 0, 0),
                           pipeline_mode=pl.Buffered(1))
    grid_spec = pltpu.PrefetchScalarGridSpec(
        num_scalar_prefetch=1, grid=(batch, nq),
        in_specs=[pl.BlockSpec((n_heads, LANES), lambda b, i, pt: (0, 0)),
                  pl.BlockSpec((1, n_pairs, blk, LANES), lambda b, i, pt: (b, 0, i, 0)),
                  kv_spec, kv_spec,
                  per_seq((n_rows, W_B)),
                  pl.BlockSpec((n_rows, LANES), lambda b, i, pt: (0, 0)),
                  per_seq((W_B, PAGE_SIZE)), per_seq((W_B, PAGE_SIZE)),
                  pl.BlockSpec(memory_space=pl.ANY), pl.BlockSpec(memory_space=pl.ANY)],
        out_specs=[pl.BlockSpec((1, blk, n_pairs * LANES), lambda b, i, pt: (b, i, 0)),
                   per_seq((n_q, W_B))],
        scratch_shapes=[pltpu.VMEM((n_heads, blk, LANES), F32), pltpu.VMEM((n_heads, blk, LANES), F32),
                        pltpu.VMEM((n_rows, W_B), F32), pltpu.VMEM((n_rows, LANES), F32),
                        pltpu.VMEM((2, pages_per_step, W_B, PAGE_SIZE), F32),
                        pltpu.VMEM((2, pages_per_step, W_B, PAGE_SIZE), F32),
                        pltpu.SemaphoreType.DMA((2,))])
    return pl.pallas_call(
        functools.partial(_sb_kernel, blk=blk, pages_per_step=pages_per_step,
                          steps_per_seq=steps_per_seq, n_pages=n_pages, n_q=n_q),
        grid_spec=grid_spec,
        out_shape=[jax.ShapeDtypeStruct((batch, seq_len, n_pairs * LANES), BF16),
                   jax.ShapeDtypeStruct((n_seq, n_q, W_B), F32)],
        compiler_params=_cparams(2), name="sb_attention",
    )(page_table, _bias_rows(bias2), q_hm, k_hm, v_hm, qbd, bias_tile, k_new, v_new, pool_k, pool_v)


def _mixer_out_ln(x_ref, a1_ref, a2_ref, wo_ref, g_ref, b_ref):
    a = jnp.concatenate([a1_ref[...], a2_ref[...]], axis=1)
    y = jnp.dot(a, wo_ref[...], preferred_element_type=F32)
    return _ln(ALPHA * x_ref[...] + y, g_ref[...], b_ref[...])


def _mixer_specs(tm, row_map, k1, a2_block_col, w_shape):
    i0 = lambda *idx: (row_map(*idx), 0)
    i2 = lambda *idx: (row_map(*idx), a2_block_col)
    return [pl.BlockSpec((tm, D_MODEL), i0), pl.BlockSpec((tm, k1), i0), pl.BlockSpec((tm, k1), i2),
            _const_spec(w_shape), _const_spec((1, D_MODEL)), _const_spec((1, D_MODEL))]


def _ffn_chunk(x_bf, wup_ref, cw_ref, cb_ref, ci, shift_fix):
    cols = slice(ci * FF_CHUNK, (ci + 1) * FF_CHUNK)
    vcols = slice(D_FF + ci * FF_CHUNK, D_FF + (ci + 1) * FF_CHUNK)
    g = jnp.dot(x_bf, wup_ref[:, cols], preferred_element_type=F32)
    val = jnp.dot(x_bf, wup_ref[:, vcols], preferred_element_type=F32)
    g1, g2 = shift_fix(g, cols)
    cw = cw_ref[:, cols]
    conv = cb_ref[:, cols] + cw[0:1] * g2 + cw[1:2] * g1 + cw[2:3] * g
    return g, (jax.nn.gelu(conv) * val).astype(BF16)


def _ffn_prompt_kernel(x_ref, a1_ref, a2_ref, wo_ref, g1_ref, b1_ref,
                       wup_ref, cw_ref, cb_ref, wdn_ref, g_ref, b_ref,
                       y_ref, conv_ref, act_ref, prev_ref):
    tm = x_ref.shape[0]

    @pl.when(pl.program_id(1) == 0)
    def _():
        prev_ref[...] = jnp.zeros_like(prev_ref)

    x = _mixer_out_ln(x_ref, a1_ref, a2_ref, wo_ref, g1_ref, b1_ref)
    x_bf = x.astype(BF16)
    row = lax.broadcasted_iota(jnp.int32, (tm, FF_CHUNK), 0)

    def shift_fix(g, cols):
        p0 = prev_ref[0:1, cols]
        p1 = prev_ref[1:2, cols]
        g1 = jnp.where(row == 0, p1, pltpu.roll(g, 1, 0))
        g2 = jnp.where(row == 0, p0, jnp.where(row == 1, p1, pltpu.roll(g, 2, 0)))
        return g1, g2

    for ci in range(D_FF // FF_CHUNK):
        cols = slice(ci * FF_CHUNK, (ci + 1) * FF_CHUNK)
        g, act = _ffn_chunk(x_bf, wup_ref, cw_ref, cb_ref, ci, shift_fix)
        act_ref[:, cols] = act
        prev_ref[0:2, cols] = g[tm - 2:tm]
        conv_ref[0, :, cols] = g[tm - 2:tm]
    for rows in (slice(0, tm // 2), slice(tm // 2, tm)):
        y = jnp.dot(act_ref[rows, :], wdn_ref[...], preferred_element_type=F32)
        y_ref[rows, :] = _ln(ALPHA * x[rows] + y, g_ref[...], b_ref[...])


def _ffn_sample_kernel(x_ref, a1_ref, a2_ref, wo_ref, g1_ref, b1_ref,
                       wup_ref, cw_ref, cb_ref, wdn_ref, g_ref, b_ref, inj1_ref, inj2_ref,
                       y_ref, gate_ref, act_ref, *, seq_len):
    tm = x_ref.shape[0]
    x = _mixer_out_ln(x_ref, a1_ref, a2_ref, wo_ref, g1_ref, b1_ref)
    x_bf = x.astype(BF16)
    pos = lax.broadcasted_iota(jnp.int32, (tm, FF_CHUNK), 0) % seq_len

    def shift_fix(g, cols):
        g1 = jnp.where(pos >= 1, pltpu.roll(g, 1, 0), inj1_ref[:, cols])
        g2 = jnp.where(pos >= 2, pltpu.roll(g, 2, 0), inj2_ref[:, cols])
        return g1, g2

    for ci in range(D_FF // FF_CHUNK):
        cols = slice(ci * FF_CHUNK, (ci + 1) * FF_CHUNK)
        g, act = _ffn_chunk(x_bf, wup_ref, cw_ref, cb_ref, ci, shift_fix)
        act_ref[:, cols] = act
        gate_ref[:, cols] = g
    y = jnp.dot(act_ref[...], wdn_ref[...], preferred_element_type=F32)
    y_ref[...] = _ln(ALPHA * x + y, g_ref[...], b_ref[...])


def _ffn_weight_specs():
    return [_const_spec((D_MODEL, 2 * D_FF)), _const_spec((CONV_W, D_FF)), _const_spec((1, D_FF)),
            _const_spec((D_FF, D_MODEL)), _const_spec((1, D_MODEL)), _const_spec((1, D_MODEL))]


def _ffn_prompt(mixer, wup_bf, cw, cb, wdn_bf, g, b, *, tm, seq_len):
    x2d, a1, a2, a2_col, wo, g1, b1 = mixer
    rows = x2d.shape[0]
    batch = rows // seq_len
    tps = seq_len // tm
    row_map = lambda bi, t: bi * tps + t
    return pl.pallas_call(
        _ffn_prompt_kernel, grid=(batch, tps),
        in_specs=_mixer_specs(tm, row_map, wo.shape[0] // 2, a2_col, wo.shape) + _ffn_weight_specs(),
        out_specs=[pl.BlockSpec((tm, D_MODEL), lambda bi, t: (bi * tps + t, 0)),
                   pl.BlockSpec((1, CONV_W - 1, D_FF), lambda bi, t: (bi, 0, 0))],
        out_shape=[jax.ShapeDtypeStruct((rows, D_MODEL), F32),
                   jax.ShapeDtypeStruct((batch, CONV_W - 1, D_FF), F32)],
        scratch_shapes=[pltpu.VMEM((tm, D_FF), BF16), pltpu.VMEM((8, D_FF), F32)],
        compiler_params=_cparams(2), name="ffn_prompt",
    )(x2d, a1, a2, wo, g1, b1, wup_bf, cw, cb, wdn_bf, g, b)


def _ffn_sample(mixer, wup_bf, cw, cb, wdn_bf, g, b, conv_state, *, seq_len):
    x2d, a1, a2, a2_col, wo, g1, b1 = mixer
    rows = x2d.shape[0]
    n_seq = rows // seq_len
    zeros = jnp.zeros((n_seq, 1, D_FF), F32)
    inj1 = jnp.concatenate([conv_state[:, 1:2]] + [zeros] * (seq_len - 1), axis=1).reshape(rows, D_FF)
    inj2 = jnp.concatenate([conv_state] + [zeros] * (seq_len - 2), axis=1).reshape(rows, D_FF)
    full = lambda width: pl.BlockSpec((rows, width), lambda i: (0, 0))
    y, gate = pl.pallas_call(
        functools.partial(_ffn_sample_kernel, seq_len=seq_len), grid=(1,),
        in_specs=(_mixer_specs(rows, lambda i: 0, wo.shape[0] // 2, a2_col, wo.shape)
                  + _ffn_weight_specs() + [full(D_FF), full(D_FF)]),
        out_specs=[full(D_MODEL), full(D_FF)],
        out_shape=[jax.ShapeDtypeStruct((rows, D_MODEL), F32), jax.ShapeDtypeStruct((rows, D_FF), F32)],
        scratch_shapes=[pltpu.VMEM((rows, D_FF), BF16)],
        compiler_params=_cparams(1), name="ffn_sample",
    )(x2d, a1, a2, wo, g1, b1, wup_bf, cw, cb, wdn_bf, g, b, inj1, inj2)
    new_state = gate.reshape(n_seq, seq_len, D_FF)[:, seq_len - (CONV_W - 1):]
    return y, new_state


def _ret_kernel(x_ref, w_ref, cos_ref, sin_ref, gg_ref, gb_ref, *refs, decay_len, has_state):
    if has_state:
        s0_ref, y_ref, sout_ref, s_ref = refs
    else:
        y_ref, sout_ref, s_ref = refs
    tm = x_ref.shape[0]
    t = pl.program_id(1)

    @pl.when(t == 0)
    def _():
        if has_state:
            s_ref[...] = s0_ref[0]
        else:
            s_ref[...] = jnp.zeros_like(s_ref)

    x = x_ref[...].astype(BF16)
    cos = cos_ref[...]
    sin = sin_ref[...]
    half = DK_C // 2

    def rotary(a):
        a1, a2 = a[:, :half], a[:, half:]
        return jnp.concatenate([a1 * cos - a2 * sin, a1 * sin + a2 * cos], axis=1)

    ii = lax.broadcasted_iota(jnp.int32, (RET_CHUNK, RET_CHUNK), 0)
    jj = lax.broadcasted_iota(jnp.int32, (RET_CHUNK, RET_CHUNK), 1)
    diff = jnp.where(ii >= jj, ii - jj, 0).astype(F32)
    idx = lax.broadcasted_iota(jnp.int32, (RET_CHUNK, 1), 0).astype(F32)
    idx_tile = (lax.broadcasted_iota(jnp.int32, (tm, 1), 0) % RET_CHUNK).astype(F32)
    v_off = 2 * H_C * DK_C
    g_off = v_off + H_C * DV_C
    n_chunks = tm // RET_CHUNK
    heads = [dict() for _ in range(H_C)]

    def project(h):
        q = rotary(jnp.dot(x, w_ref[:, h * DK_C:(h + 1) * DK_C], preferred_element_type=F32))
        k = rotary(jnp.dot(x, w_ref[:, (H_C + h) * DK_C:(H_C + h + 1) * DK_C],
                           preferred_element_type=F32)) * (DK_C ** -0.5)
        v = jnp.dot(x, w_ref[:, v_off + h * DV_C:v_off + (h + 1) * DV_C], preferred_element_type=F32)
        k_dec = jnp.exp((decay_len - 1.0 - idx_tile) * _LOG_G[h])
        heads[h].update(q=q.astype(BF16), k=k.astype(BF16), v=v.astype(BF16),
                        kd_t=(k * k_dec).T.astype(BF16))

    def project_gate(h):
        heads[h]["gate"] = jnp.dot(x, w_ref[:, g_off + h * DV_C:g_off + (h + 1) * DV_C],
                                   preferred_element_type=F32)

    chunks = [slice(ci * RET_CHUNK, (ci + 1) * RET_CHUNK) for ci in range(n_chunks)]

    def advance_state(h):
        c_dec = math.exp(decay_len * _LOG_G[h])
        v, kd_t = heads[h]["v"], heads[h].pop("kd_t")
        kvs = [jnp.dot(kd_t[:, rows], v[rows], preferred_element_type=F32) for rows in chunks]
        states = [s_ref[h]]
        for kv in kvs:
            states.append(states[-1] * c_dec + kv)
        s_ref[h] = states[-1]
        heads[h]["states"] = [state.astype(BF16) for state in states[:-1]]

    def retain(h):
        log_g = _LOG_G[h]
        dmat = jnp.where(ii >= jj, jnp.exp(diff * log_g), 0.0)
        q_dec = jnp.exp((idx + 1.0) * log_g)
        q, k, v, states = (heads[h].pop(name) for name in ("q", "k", "v", "states"))
        outs = [None] * n_chunks
        vals = [dict() for _ in chunks]

        def scores(ci):
            vals[ci]["sc"] = lax.dot_general(q[chunks[ci]], k[chunks[ci]], (((1,), (1,)), ((), ())),
                                             preferred_element_type=F32)

        def decay(ci):
            vals[ci]["sc"] = (vals[ci]["sc"] * dmat).astype(BF16)

        def output(ci):
            rows = chunks[ci]
            outs[ci] = (jnp.dot(vals[ci].pop("sc"), v[rows], preferred_element_type=F32)
                        + jnp.dot(q[rows], states[ci], preferred_element_type=F32) * q_dec)

        _emit_skewed([[functools.partial(stage, ci) for stage in (scores, decay, output)]
                      for ci in range(n_chunks)])
        heads[h]["o"] = outs

    def finish(h):
        vcols = slice(h * DV_C, (h + 1) * DV_C)
        gate = heads[h].pop("gate")
        for ci, o in enumerate(heads[h].pop("o")):
            rows = slice(ci * RET_CHUNK, (ci + 1) * RET_CHUNK)
            mu = jnp.mean(o, axis=-1, keepdims=True)
            oc = o - mu
            var = jnp.mean(oc * oc, axis=-1, keepdims=True)
            normed = oc * lax.rsqrt(var + LN_EPS) * gg_ref[:, vcols] + gb_ref[:, vcols]
            y_ref[rows, vcols] = (jax.nn.silu(gate[rows]) * normed).astype(y_ref.dtype)

    project(0)
    for h in range(H_C):
        advance_state(h)
        if h + 1 < H_C:
            project(h + 1)
        retain(h)
        project_gate(h)
        if h > 0:
            finish(h - 1)
    finish(H_C - 1)

    @pl.when(t == pl.num_programs(1) - 1)
    def _():
        sout_ref[0] = s_ref[...]


def _retention(x2d, w_bf, cos, sin, gg, gb, state0, *, tm, seq_len, decay_len, name):
    rows = x2d.shape[0]
    n_seq = rows // seq_len
    tps = seq_len // tm
    d_y = H_C * DV_C
    has_state = state0 is not None
    state_spec = pl.BlockSpec((1, H_C, DK_C, DV_C), lambda bi, t: (bi, 0, 0, 0))
    in_specs = [pl.BlockSpec((tm, D_MODEL), lambda bi, t: (bi * tps + t, 0)),
                _const_spec(w_bf.shape),
                pl.BlockSpec((tm, DK_C // 2), lambda bi, t: (t, 0)),
                pl.BlockSpec((tm, DK_C // 2), lambda bi, t: (t, 0)),
                _const_spec((1, d_y)), _const_spec((1, d_y))]
    args = [x2d, w_bf, cos, sin, gg, gb]
    if has_state:
        in_specs.append(state_spec)
        args.append(state0)
    return pl.pallas_call(
        functools.partial(_ret_kernel, decay_len=decay_len, has_state=has_state),
        grid=(n_seq, tps), in_specs=in_specs,
        out_specs=[pl.BlockSpec((tm, d_y), lambda bi, t: (bi * tps + t, 0)), state_spec],
        out_shape=[jax.ShapeDtypeStruct((rows, d_y), BF16),
                   jax.ShapeDtypeStruct((n_seq, H_C, DK_C, DV_C), F32)],
        scratch_shapes=[pltpu.VMEM((H_C, DK_C, DV_C), F32)],
        compiler_params=_cparams(2), name=name,
    )(*args)


def _rope_tables(pos):
    half = DK_C // 2
    inv = ROPE_BASE ** (-jnp.arange(half, dtype=F32) / half)
    ang = pos.astype(F32)[:, None] * inv[None, :]
    return jnp.cos(ang), jnp.sin(ang)


def kernel(x_prompt, x_sample, cache_sb_k, cache_sb_v, state_ret, state_ffn_conv, page_table,
           w_in_ab, vln_g, vln_b, w_s, b_s, sb_bias, w_out_ab, w_in_ret, gn_g, gn_b, w_out_ret,
           ln1_g, ln1_b, ln2_g, ln2_b, w_up, conv_w, conv_b, w_down):
    bp, tp, _ = x_prompt.shape
    bs, ts, _ = x_sample.shape
    n_pages = page_table.shape[1]
    past_len = n_pages * PAGE_SIZE
    n_phys = cache_sb_k.shape[1]
    xp = x_prompt.reshape(bp * tp, D_MODEL)
    xs = x_sample.reshape(bs * ts, D_MODEL)
    row2d = lambda a: a.reshape(1, -1)

    sb_k_p, sb_v_p, sb_k_s, sb_v_s, chunk_v_s = [], [], [], [], []
    ret_p, ret_s, conv_p, conv_s = [], [], [], []
    for l in range(DEPTH):
        if l % 2 == 0:
            e = l // 2
            w_in = w_in_ab[e].astype(BF16)
            vg, vb = row2d(vln_g[e]), row2d(vln_b[e])
            mixb_p = jnp.repeat(b_s[e].T, DG_A, axis=1)
            a_p, q_hm, k_hm, v_hm, kf_p, vf_p = _ab_proj(
                xp, w_in, vg, vb, w_s[e], mixb_p, mix_len=CHUNK, head_major=True, tm=ROW_TILE, seq_len=tp)
            bias2 = sb_bias[e] * LOG2E

            reps = CHUNK // ts
            wmix_s = jnp.tile(w_s[e][:, :ts, :ts], (1, reps, reps))
            mixb_s = jnp.repeat(jnp.tile(b_s[e][:, :ts], (1, reps)).T, DG_A, axis=1)
            a_s, q_s, kf_s, vf_s, va_s = _ab_proj(
                xs, w_in, vg, vb, wmix_s, mixb_s, mix_len=ts, head_major=False, tm=bs * ts)
            head_of_col = jnp.arange(W_B) // DH_B
            own = (jnp.arange(H_B)[:, None] == head_of_col[None, :]).astype(F32)
            qbd = (q_s.reshape(bs, ts, 1, W_B) * own[None, None]).reshape(bs, ts * H_B, W_B).astype(BF16)
            bias_tile = jnp.broadcast_to(jnp.tile(bias2, ts)[:, None], (ts * H_B, LANES))
            pad = ((0, 0), (0, 0), (0, PAGE_SIZE - ts))
            k_new = jnp.pad(kf_s.reshape(bs, ts, W_B).transpose(0, 2, 1), pad)
            v_new = jnp.pad(vf_s.reshape(bs, ts, W_B).transpose(0, 2, 1), pad)
            pool_k = cache_sb_k[e].transpose(0, 2, 3, 1).reshape(n_phys, W_B, PAGE_SIZE)
            pool_v = cache_sb_v[e].transpose(0, 2, 3, 1).reshape(n_phys, W_B, PAGE_SIZE)
            b_p, b_s_out = _sb_attention(q_hm, k_hm, v_hm, bias2, page_table, qbd, bias_tile,
                                         k_new, v_new, pool_k, pool_v, blk=ATTN_BLOCK)

            w_out = w_out_ab[e].astype(BF16)
            g1, b1 = row2d(ln1_g[l]), row2d(ln1_b[l])
            mix_p = (xp, a_p, b_p.reshape(bp * tp, W_B), 0, w_out, g1, b1)
            mix_s = (xs, a_s, b_s_out.reshape(bs * ts, W_B).astype(BF16), 0, w_out, g1, b1)
            pages_of = lambda a: a.reshape(bp, tp // PAGE_SIZE, H_B, DH_B, PAGE_SIZE).transpose(0, 1, 4, 2, 3)
            sb_k_p.append(pages_of(kf_p))
            sb_v_p.append(pages_of(vf_p))
            sb_k_s.append(kf_s.reshape(bs, ts, H_B, DH_B))
            sb_v_s.append(vf_s.reshape(bs, ts, H_B, DH_B))
            chunk_v_s.append(va_s.reshape(bs, ts, G_A, DG_A))
        else:
            o = l // 2
            w_in = w_in_ret[o].astype(BF16)
            gg, gb = row2d(gn_g[o]), row2d(gn_b[o])
            cos_p, sin_p = _rope_tables(jnp.arange(tp))
            y_p, s_p = _retention(xp, w_in, cos_p, sin_p, gg, gb, None, tm=ROW_TILE, seq_len=tp,
                                  decay_len=RET_CHUNK, name="retention_prompt")
            cos_s, sin_s = _rope_tables(past_len + jnp.arange(RET_CHUNK))
            xs_pad = jnp.pad(xs.reshape(bs, ts, D_MODEL), ((0, 0), (0, RET_CHUNK - ts), (0, 0)))
            y_s, s_s = _retention(xs_pad.reshape(bs * RET_CHUNK, D_MODEL), w_in, cos_s, sin_s, gg, gb,
                                  state_ret[o], tm=RET_CHUNK, seq_len=RET_CHUNK, decay_len=ts,
                                  name="retention_sample")
            y_s = y_s.reshape(bs, RET_CHUNK, H_C * DV_C)[:, :ts].reshape(bs * ts, H_C * DV_C)
            w_out = w_out_ret[o].astype(BF16)
            g1, b1 = row2d(ln1_g[l]), row2d(ln1_b[l])
            mix_p = (xp, y_p, y_p, 1, w_out, g1, b1)
            mix_s = (xs, y_s, y_s, 1, w_out, g1, b1)
            ret_p.append(s_p)
            ret_s.append(s_s)

        wup = w_up[l].astype(BF16)
        wdn = w_down[l].astype(BF16)
        g2, b2 = row2d(ln2_g[l]), row2d(ln2_b[l])
        xp, cp = _ffn_prompt(mix_p, wup, conv_w[l], row2d(conv_b[l]), wdn, g2, b2, tm=ROW_TILE, seq_len=tp)
        xs, cs = _ffn_sample(mix_s, wup, conv_w[l], row2d(conv_b[l]), wdn, g2, b2, state_ffn_conv[l], seq_len=ts)
        conv_p.append(cp)
        conv_s.append(cs)

    return (xp.reshape(bp, tp, D_MODEL), xs.reshape(bs, ts, D_MODEL),
            jnp.stack(sb_k_p), jnp.stack(sb_v_p), jnp.stack(sb_k_s), jnp.stack(sb_v_s),
            jnp.stack(chunk_v_s), jnp.stack(ret_p), jnp.stack(ret_s), jnp.stack(conv_p), jnp.stack(conv_s))
```
